```python
import jax, jax.numpy as jnp
from jax import lax
import numpy as np

D_MODEL = 1024
BATCH = 4
SEQ = 4096
DEPTH = 2
DEC_BATCH = 8
DEC_SEQ = 16
PAST_LEN = 2048

CHUNK = 64
Q_BLOCK = 128
N_MEM = 256

MLA_HEADS = 4
MLA_NOPE = 128
MLA_ROPE = 64
MLA_V = 128
MLA_Q_RANK = 384
MLA_KV_RANK = 256
MLA_WIDTH = MLA_HEADS * MLA_V
FOX_HEADS = 4
FOX_DIM = 64
FOX_WIDTH = FOX_HEADS * FOX_DIM
MEM_HEADS = 4
MEM_DIM = 64
MEM_WIDTH = MEM_HEADS * MEM_DIM
D_MIX = MLA_WIDTH + FOX_WIDTH + MEM_WIDTH

ROPE_THETA = 10000.0
NORM_EPS = 1e-6
NEG_INF = -1e30
MLA_SCALE = (MLA_NOPE + MLA_ROPE) ** -0.5
FOX_SCALE = FOX_DIM ** -0.5
MEM_SCALE = MEM_DIM ** -0.5
DEEPNORM_ALPHA = (2 * DEPTH) ** 0.25
DEEPNORM_BETA = (8 * DEPTH) ** -0.25

IN_SPLITS = (MLA_Q_RANK, MLA_KV_RANK, MLA_ROPE, MLA_WIDTH,
             FOX_WIDTH, FOX_WIDTH, FOX_WIDTH, FOX_HEADS, FOX_WIDTH,
             MEM_WIDTH, MEM_WIDTH)
D_IN = sum(IN_SPLITS)

kernel_name = 'hybrid_mla_fox_mem_streaming_step'


def _rms_norm(x, g):
    xf = x.astype(jnp.float32)
    y = xf * lax.rsqrt(jnp.mean(xf * xf, axis=-1, keepdims=True) + NORM_EPS)
    return (y * g.astype(jnp.float32)).astype(x.dtype)


def _layer_norm(x, g, b):
    xf = x.astype(jnp.float32)
    mu = jnp.mean(xf, axis=-1, keepdims=True)
    xc = xf - mu
    var = jnp.mean(xc * xc, axis=-1, keepdims=True)
    y = xc * lax.rsqrt(var + NORM_EPS) * g.astype(jnp.float32) + b.astype(jnp.float32)
    return y.astype(x.dtype)


def _rope(x, pos):
    half = x.shape[-1] // 2
    inv = ROPE_THETA ** (-jnp.arange(half, dtype=jnp.float32) / half)
    ang = pos.astype(jnp.float32)[:, None] * inv[None, :]
    shape = (1, pos.shape[0]) + (1,) * (x.ndim - 3) + (half,)
    cos = jnp.cos(ang).reshape(shape)
    sin = jnp.sin(ang).reshape(shape)
    x1 = x[..., :half].astype(jnp.float32)
    x2 = x[..., half:].astype(jnp.float32)
    return jnp.concatenate([x1 * cos - x2 * sin, x1 * sin + x2 * cos], axis=-1).astype(x.dtype)


def _attend(q, k, v, q_pos, k_pos, scale, chunk_causal, cq=None, ck=None):
    B, Tq, H, _ = q.shape
    Dv = v.shape[-1]
    k_idx = k_pos // CHUNK if chunk_causal else k_pos
    ck_t = None if ck is None else jnp.transpose(ck, (0, 2, 1))

    def block(args):
        qb, pb, cqb = args
        s = jnp.einsum('bqhd,bkhd->bhqk', qb, k).astype(jnp.float32) * scale
        if cqb is not None:
            s = s + (jnp.transpose(cqb, (0, 2, 1))[:, :, :, None] - ck_t[:, :, None, :])
        q_idx = pb // CHUNK if chunk_causal else pb
        allowed = k_idx[None, :] <= q_idx[:, None]
        s = jnp.where(allowed[None, None], s, NEG_INF)
        p = jax.nn.softmax(s, axis=-1).astype(v.dtype)
        return jnp.einsum('bhqk,bkhd->bqhd', p, v)

    if Tq > Q_BLOCK and Tq % Q_BLOCK == 0:
        nb = Tq // Q_BLOCK
        qs = jnp.transpose(q.reshape(B, nb, Q_BLOCK, H, q.shape[-1]), (1, 0, 2, 3, 4))
        ps = q_pos.reshape(nb, Q_BLOCK)
        cs = None if cq is None else jnp.transpose(cq.reshape(B, nb, Q_BLOCK, H), (1, 0, 2, 3))
        out = lax.map(block, (qs, ps, cs))
        return jnp.transpose(out, (1, 0, 2, 3, 4)).reshape(B, Tq, H, Dv)
    return block((q, q_pos, cq))


def _mem_kv(mem, w_mem_kv):
    B, M, _ = mem.shape
    kv = jnp.einsum('bmd,de->bme', mem, w_mem_kv)
    mk = kv[..., :MEM_WIDTH].reshape(B, M, MEM_HEADS, MEM_DIM)
    mv = kv[..., MEM_WIDTH:].reshape(B, M, MEM_HEADS, MEM_DIM)
    return mk, mv


def _layer(x, pos, mem_k, mem_v, past, w_in, b_f, q_norm_g, kv_norm_g, w_uq, w_ukv, w_out, ln_g, ln_b):
    B, T, _ = x.shape
    h = jnp.einsum('btd,de->bte', x, w_in)
    split_idx = np.cumsum(IN_SPLITS)[:-1].tolist()
    (c_q, c_kv, k_rope, g_mla, f_q, f_k, f_v, f_logit, g_fox, m_q, g_mem) = jnp.split(h, split_idx, axis=-1)

    q = jnp.einsum('btr,re->bte', _rms_norm(c_q, q_norm_g), w_uq).reshape(B, T, MLA_HEADS, MLA_NOPE + MLA_ROPE)
    q = jnp.concatenate([q[..., :MLA_NOPE], _rope(q[..., MLA_NOPE:], pos)], axis=-1)
    lat_new = _rms_norm(c_kv, kv_norm_g)
    kr_new = _rope(k_rope, pos)

    fq = f_q.reshape(B, T, FOX_HEADS, FOX_DIM)
    fk_new = f_k.reshape(B, T, FOX_HEADS, FOX_DIM)
    fv_new = f_v.reshape(B, T, FOX_HEADS, FOX_DIM)
    logf_new = jax.nn.log_sigmoid(f_logit.astype(jnp.float32) + b_f.astype(jnp.float32))

    if past is None:
        lat, kr, fk, fv, logf, k_pos = lat_new, kr_new, fk_new, fv_new, logf_new, pos
    else:
        p_lat, p_kr, p_fk, p_fv, p_logf = past
        lat = jnp.concatenate([p_lat, lat_new], axis=1)
        kr = jnp.concatenate([p_kr, kr_new], axis=1)
        fk = jnp.concatenate([p_fk, fk_new], axis=1)
        fv = jnp.concatenate([p_fv, fv_new], axis=1)
        logf = jnp.concatenate([p_logf.astype(jnp.float32), logf_new], axis=1)
        k_pos = jnp.arange(lat.shape[1])
    Tk = lat.shape[1]

    kv = jnp.einsum('btr,re->bte', lat, w_ukv).reshape(B, Tk, MLA_HEADS, MLA_NOPE + MLA_V)
    k_mla = jnp.concatenate([kv[..., :MLA_NOPE],
                             jnp.broadcast_to(kr[:, :, None, :], (B, Tk, MLA_HEADS, MLA_ROPE))], axis=-1)
    o_mla = _attend(q, k_mla, kv[..., MLA_NOPE:], pos, k_pos, MLA_SCALE, True)

    cum = jnp.cumsum(logf, axis=1)
    o_fox = _attend(fq, fk, fv, pos, k_pos, FOX_SCALE, False, cum[:, Tk - T:], cum)

    mq = m_q.reshape(B, T, MEM_HEADS, MEM_DIM)
    s_mem = jnp.einsum('bthd,bmhd->bhtm', mq, mem_k).astype(jnp.float32) * MEM_SCALE
    o_mem = jnp.einsum('bhtm,bmhd->bthd', jax.nn.softmax(s_mem, axis=-1).astype(mem_v.dtype), mem_v)

    mixed = jnp.concatenate([o_mla.reshape(B, T, MLA_WIDTH) * jax.nn.silu(g_mla),
                             o_fox.reshape(B, T, FOX_WIDTH) * jax.nn.silu(g_fox),
                             o_mem.reshape(B, T, MEM_WIDTH) * jax.nn.silu(g_mem)], axis=-1)
    out = jnp.einsum('bte,ed->btd', mixed, w_out)
    y = _layer_norm(DEEPNORM_ALPHA * x + out, ln_g, ln_b)
    return y, (lat_new, kr_new, fk_new, fv_new, logf_new)


def setup_inputs(seed: int = 0) -> dict:
    key = jax.random.key(seed)
    ks = jax.random.split(key, 24)
    f32 = jnp.float32
    nrm = lambda k, shape: jax.random.normal(k, shape, dtype=f32)
    return {
        'x_prompt': nrm(ks[0], (BATCH, SEQ, D_MODEL)),
        'x_sample': nrm(ks[1], (DEC_BATCH, DEC_SEQ, D_MODEL)),
        'cache_mla_latent': nrm(ks[2], (DEPTH, DEC_BATCH, PAST_LEN, MLA_KV_RANK)),
        'cache_mla_krope': nrm(ks[3], (DEPTH, DEC_BATCH, PAST_LEN, MLA_ROPE)),
        'cache_fox_k': nrm(ks[4], (DEPTH, DEC_BATCH, PAST_LEN, FOX_HEADS, FOX_DIM)),
        'cache_fox_v': nrm(ks[5], (DEPTH, DEC_BATCH, PAST_LEN, FOX_HEADS, FOX_DIM)),
        'cache_fox_logf': jax.nn.log_sigmoid(3.0 + nrm(ks[6], (DEPTH, DEC_BATCH, PAST_LEN, FOX_HEADS))),
        'cache_mem_k': nrm(ks[7], (DEPTH, DEC_BATCH, N_MEM, MEM_HEADS, MEM_DIM)),
        'cache_mem_v': nrm(ks[8], (DEPTH, DEC_BATCH, N_MEM, MEM_HEADS, MEM_DIM)),
        'mem_prompt': nrm(ks[9], (BATCH, N_MEM, D_MODEL)),
        'w_in': nrm(ks[10], (DEPTH, D_MODEL, D_IN)) * D_MODEL ** -0.5,
        'b_fox_f': 3.0 + 0.5 * nrm(ks[11], (DEPTH, FOX_HEADS)),
        'mla_q_norm': 1.0 + 0.01 * nrm(ks[12], (DEPTH, MLA_Q_RANK)),
        'mla_kv_norm': 1.0 + 0.01 * nrm(ks[13], (DEPTH, MLA_KV_RANK)),
        'w_uq': nrm(ks[14], (DEPTH, MLA_Q_RANK, MLA_HEADS * (MLA_NOPE + MLA_ROPE))) * MLA_Q_RANK ** -0.5,
        'w_ukv': nrm(ks[15], (DEPTH, MLA_KV_RANK, MLA_HEADS * (MLA_NOPE + MLA_V))) * MLA_KV_RANK ** -0.5,
        'w_mem_kv': nrm(ks[16], (DEPTH, D_MODEL, 2 * MEM_WIDTH)) * D_MODEL ** -0.5,
        'w_out': nrm(ks[17], (DEPTH, D_MIX, D_MODEL)) * (D_MIX ** -0.5 * DEEPNORM_BETA),
        'ln_g': 1.0 + 0.01 * nrm(ks[18], (DEPTH, D_MODEL)),
        'ln_b': 0.01 * nrm(ks[19], (DEPTH, D_MODEL)),
    }


def reference(x_prompt, x_sample, cache_mla_latent, cache_mla_krope, cache_fox_k, cache_fox_v, cache_fox_logf,
              cache_mem_k, cache_mem_v, mem_prompt, w_in, b_fox_f, mla_q_norm, mla_kv_norm, w_uq, w_ukv,
              w_mem_kv, w_out, ln_g, ln_b):
    past_len = cache_mla_latent.shape[2]
    pos_p = jnp.arange(x_prompt.shape[1])
    pos_s = past_len + jnp.arange(x_sample.shape[1])
    yp, ys = x_prompt, x_sample
    p_rows, s_rows, p_mk, p_mv = [], [], [], []
    for l in range(DEPTH):
        weights = (w_in[l], b_fox_f[l], mla_q_norm[l], mla_kv_norm[l], w_uq[l], w_ukv[l], w_out[l], ln_g[l], ln_b[l])
        mk, mv = _mem_kv(mem_prompt, w_mem_kv[l])
        yp, rp = _layer(yp, pos_p, mk, mv, None, *weights)
        past = (cache_mla_latent[l], cache_mla_krope[l], cache_fox_k[l], cache_fox_v[l], cache_fox_logf[l])
        ys, rs = _layer(ys, pos_s, cache_mem_k[l], cache_mem_v[l], past, *weights)
        p_rows.append(rp)
        s_rows.append(rs)
        p_mk.append(mk)
        p_mv.append(mv)
    p_lat, p_kr, p_fk, p_fv, p_logf = [jnp.stack([r[i] for r in p_rows], axis=0) for i in range(5)]
    s_lat, s_kr, s_fk, s_fv, s_logf = [jnp.stack([r[i] for r in s_rows], axis=0) for i in range(5)]
    p_mem_k = jnp.stack(p_mk, axis=0)
    p_mem_v = jnp.stack(p_mv, axis=0)
    return (yp, ys, p_lat, p_kr, p_fk, p_fv, p_logf, p_mem_k, p_mem_v, s_lat, s_kr, s_fk, s_fv, s_logf)
```

```python
import functools

import jax
import jax.numpy as jnp
from jax import lax
from jax.experimental import pallas as pl
from jax.experimental.pallas import tpu as pltpu

F32 = jnp.float32
BF16 = jnp.bfloat16

D_MODEL = 1024
CHUNK = 64
N_MEM = 256
HEADS = 4
MLA_NOPE = 128
MLA_ROPE = 64
MLA_V = 128
MLA_Q_RANK = 384
MLA_KV_RANK = 256
FOX_DIM = 64
MEM_DIM = 64
ROPE_THETA = 10000.0
NORM_EPS = 1e-6
NEG_INF = -1e30
MLA_SCALE = (MLA_NOPE + MLA_ROPE) ** -0.5
FOX_SCALE = FOX_DIM ** -0.5
MEM_SCALE = MEM_DIM ** -0.5
DEPTH = 2
DEEPNORM_ALPHA = (2 * DEPTH) ** 0.25

LANES = 128
MLA_DK = 256
FOX_DK = 128
BIAS_LANE = 64
VMEM_LIMIT = 52 * 1024 * 1024

O_CQ, O_CKV, O_GMLA, O_FQ, O_FK, O_FV, O_GFOX, O_MQ, O_GMEM, O_KR, O_END = (
    0, 384, 640, 1152, 1408, 1664, 1920, 2176, 2432, 2688, 2816)


def _params(sem):
    return pltpu.CompilerParams(dimension_semantics=sem, vmem_limit_bytes=VMEM_LIMIT)


def _silu(x):
    return x * jax.nn.sigmoid(x)


def _rms(x, g):
    return x * lax.rsqrt(jnp.mean(x * x, axis=-1, keepdims=True) + NORM_EPS) * g


def _rope128(x, cos, sin):
    lane = lax.broadcasted_iota(jnp.int32, x.shape, 1)
    swapped = jnp.where(lane < MLA_ROPE // 2, pltpu.roll(x, LANES - MLA_ROPE // 2, 1),
                        pltpu.roll(x, MLA_ROPE // 2, 1))
    return x * cos + swapped * sin


def _cumsum_rows(x):
    n = x.shape[0]
    row = lax.broadcasted_iota(jnp.int32, x.shape, 0)
    shift = 1
    while shift < n:
        x = x + jnp.where(row >= shift, pltpu.roll(x, shift, 0), 0.0)
        shift *= 2
    return x


def _split3(x):
    hi = x.astype(BF16).astype(F32)
    r = x - hi
    mid = r.astype(BF16).astype(F32)
    return hi, mid, r - mid


def _head_cols(x, h):
    col = x[:, (h // 2) * LANES:(h // 2 + 1) * LANES]
    return pltpu.roll(col, FOX_DIM, 1) if h % 2 else col


def _fox_kv_aug(fk, fv, cum, fka_ref, fva_ref):
    rows = fk.shape[0]
    lane = lax.broadcasted_iota(jnp.int32, (rows, LANES), 1)
    for h in range(HEADS):
        c = jnp.broadcast_to(cum[:, BIAS_LANE + h:BIAS_LANE + h + 1], (rows, LANES))
        hi, mid, lo = _split3(c)
        kb = jnp.where(lane < BIAS_LANE + 3, 1.0,
             jnp.where(lane == BIAS_LANE + 3, -hi,
             jnp.where(lane == BIAS_LANE + 4, -mid,
             jnp.where(lane == BIAS_LANE + 5, -lo, 0.0))))
        ka = jnp.where(lane < BIAS_LANE, _head_cols(fk, h), kb)
        va = jnp.where(lane < BIAS_LANE, _head_cols(fv, h),
                       jnp.where(lane == BIAS_LANE, 1.0, 0.0))
        fka_ref[:, h * LANES:(h + 1) * LANES] = ka.astype(BF16)
        fva_ref[:, h * LANES:(h + 1) * LANES] = va.astype(BF16)


def _mla_kv(lat, kr128, wukv_ref, k_ref, v_ref):
    rows = lat.shape[0]
    kv = jnp.dot(lat.astype(BF16), wukv_ref[...], preferred_element_type=F32)
    lane = lax.broadcasted_iota(jnp.int32, (rows, LANES), 1)
    krb = jnp.where(lane < MLA_ROPE, kr128, 0.0).astype(BF16)
    for h in range(HEADS):
        k_ref[:, h * MLA_DK:h * MLA_DK + MLA_NOPE] = kv[:, 2 * h * LANES:(2 * h + 1) * LANES].astype(BF16)
        k_ref[:, h * MLA_DK + MLA_NOPE:(h + 1) * MLA_DK] = krb
        v_ref[:, h * MLA_V:(h + 1) * MLA_V] = kv[:, (2 * h + 1) * LANES:(2 * h + 2) * LANES].astype(BF16)


def _proj_kernel(x_ref, win_ref, wuq_ref, wukv_ref, qg_ref, kvg_ref, bf_ref, cos_ref, sin_ref,
                 cin_ref, mk_ref, mv_ref,
                 lat_ref, kr_ref, fk_ref, fv_ref, logf_ref, q_ref, k_ref, v_ref,
                 fqa_ref, fka_ref, fva_ref, gmla_ref, gfox_ref, mmem_ref, carry_ref):
    tm = x_ref.shape[0]

    @pl.when(pl.program_id(1) == 0)
    def _():
        carry_ref[...] = cin_ref[0]

    xb = x_ref[...].astype(BF16)

    def seg(lo, hi):
        return jnp.dot(xb, win_ref[:, lo:hi], preferred_element_type=F32)

    cos = cos_ref[...]
    sin = sin_ref[...]
    lane = lax.broadcasted_iota(jnp.int32, (tm, LANES), 1)

    cqn = _rms(seg(O_CQ, O_CKV), qg_ref[...])
    q = jnp.dot(cqn.astype(BF16), wuq_ref[...], preferred_element_type=F32)
    for h in range(HEADS):
        q_ref[:, h * MLA_DK:h * MLA_DK + MLA_NOPE] = q[:, 2 * h * LANES:(2 * h + 1) * LANES].astype(BF16)
        q_ref[:, h * MLA_DK + MLA_NOPE:(h + 1) * MLA_DK] = _rope128(
            q[:, (2 * h + 1) * LANES:(2 * h + 2) * LANES], cos, sin).astype(BF16)

    lat = _rms(seg(O_CKV, O_GMLA), kvg_ref[...])
    lat_ref[...] = lat
    krl = seg(O_KR, O_END)
    kr128 = _rope128(krl, cos, sin)
    kr_ref[...] = kr128[:, :MLA_ROPE]
    _mla_kv(lat, kr128, wukv_ref, k_ref, v_ref)

    z = krl + bf_ref[...]
    logf = jnp.minimum(z, 0.0) - jnp.log1p(jnp.exp(-jnp.abs(z)))
    logf = jnp.where((lane >= BIAS_LANE) & (lane < BIAS_LANE + HEADS), logf, 0.0)
    logf_ref[...] = pltpu.roll(logf, LANES - BIAS_LANE, 1)[:, :HEADS]
    cum = _cumsum_rows(logf) + carry_ref[0:1, :]
    carry_ref[...] = jnp.broadcast_to(cum[tm - 1:tm, :], carry_ref.shape)

    fq = seg(O_FQ, O_FK)
    fk = seg(O_FK, O_FV)
    fv = seg(O_FV, O_GFOX)
    fk_ref[...] = fk
    fv_ref[...] = fv
    _fox_kv_aug(fk, fv, cum, fka_ref, fva_ref)
    gfox = _silu(seg(O_GFOX, O_MQ))
    for h in range(HEADS):
        c = jnp.broadcast_to(cum[:, BIAS_LANE + h:BIAS_LANE + h + 1], (tm, LANES))
        hi, mid, lo = _split3(c)
        qb = jnp.where(lane == BIAS_LANE, hi,
             jnp.where(lane == BIAS_LANE + 1, mid,
             jnp.where(lane == BIAS_LANE + 2, lo,
             jnp.where(lane < BIAS_LANE + 6, 1.0, 0.0))))
        qa = jnp.where(lane < BIAS_LANE, _head_cols(fq, h) * FOX_SCALE, qb)
        fqa_ref[:, h * LANES:(h + 1) * LANES] = qa.astype(BF16)
        gfox_ref[:, h * LANES:(h + 1) * LANES] = jnp.where(lane < BIAS_LANE, _head_cols(gfox, h), 0.0)

    gmla_ref[...] = _silu(seg(O_GMLA, O_FQ))

    mq = seg(O_MQ, O_GMEM).astype(BF16)
    gmem = _silu(seg(O_GMEM, O_KR))
    mk = mk_ref[0]
    mv = mv_ref[0].astype(BF16)
    key_head = lax.broadcasted_iota(jnp.int32, mk.shape, 1) // MEM_DIM
    out_head = lax.broadcasted_iota(jnp.int32, (tm, HEADS * MEM_DIM), 1) // MEM_DIM
    o_mem = jnp.zeros((tm, HEADS * MEM_DIM), F32)
    for h in range(HEADS):
        mk_h = jnp.where(key_head == h, mk, 0.0).astype(BF16)
        s = lax.dot_general(mq, mk_h, (((1,), (1,)), ((), ())), preferred_element_type=F32) * MEM_SCALE
        e = jnp.exp(s - jnp.max(s, axis=-1, keepdims=True))
        p = e / jnp.sum(e, axis=-1, keepdims=True)
        o_h = jnp.dot(p.astype(BF16), mv, preferred_element_type=F32)
        o_mem = jnp.where(out_head == h, o_h, o_mem)
    mmem_ref[...] = (o_mem * gmem).astype(BF16)


def _proj(x, cin, mk, mv, cos, sin, w, *, batch, seq, tm):
    rows = batch * seq
    nt = seq // tm
    row_map = lambda b, i: (b * nt + i, 0)
    full = lambda b, i: (0, 0)
    per_b = lambda b, i: (b, 0, 0)
    tab_map = lambda b, i: (i, 0)

    def rows_spec(width):
        return pl.BlockSpec((tm, width), row_map)

    def full_spec(a):
        return pl.BlockSpec(a.shape, full)

    out_widths = [(MLA_KV_RANK, F32), (MLA_ROPE, F32), (HEADS * FOX_DIM, F32), (HEADS * FOX_DIM, F32),
                  (HEADS, F32), (HEADS * MLA_DK, BF16), (HEADS * MLA_DK, BF16), (HEADS * MLA_V, BF16),
                  (HEADS * FOX_DK, BF16), (HEADS * FOX_DK, BF16), (HEADS * FOX_DK, BF16),
                  (HEADS * MLA_V, F32), (HEADS * FOX_DK, F32), (HEADS * MEM_DIM, BF16)]
    return pl.pallas_call(
        _proj_kernel,
        grid=(batch, nt),
        in_specs=[rows_spec(D_MODEL), full_spec(w['win']), full_spec(w['wuq']), full_spec(w['wukv']),
                  full_spec(w['qg']), full_spec(w['kvg']), full_spec(w['bf']),
                  pl.BlockSpec((tm, LANES), tab_map), pl.BlockSpec((tm, LANES), tab_map),
                  pl.BlockSpec((1, 8, LANES), per_b),
                  pl.BlockSpec((1, N_MEM, HEADS * MEM_DIM), per_b),
                  pl.BlockSpec((1, N_MEM, HEADS * MEM_DIM), per_b)],
        out_specs=[rows_spec(wd) for wd, _ in out_widths],
        out_shape=[jax.ShapeDtypeStruct((rows, wd), dt) for wd, dt in out_widths],
        scratch_shapes=[pltpu.VMEM((8, LANES), F32)],
        compiler_params=_params(("arbitrary", "arbitrary")),
        name="proj",
    )(x, w['win'], w['wuq'], w['wukv'], w['qg'], w['kvg'], w['bf'], cos, sin, cin, mk, mv)


def _cache_kernel(lat_ref, kr_ref, fk_ref, fv_ref, logf_ref, wukv_ref,
                  k_ref, v_ref, fka_ref, fva_ref, cout_ref, carry_ref):
    tm = lat_ref.shape[0]

    @pl.when(pl.program_id(1) == 0)
    def _():
        carry_ref[...] = jnp.zeros_like(carry_ref)

    kr128 = jnp.concatenate([kr_ref[...], jnp.zeros((tm, LANES - MLA_ROPE), F32)], axis=1)
    _mla_kv(lat_ref[...], kr128, wukv_ref, k_ref, v_ref)
    cum = _cumsum_rows(logf_ref[...]) + carry_ref[0:1, :]
    carry_ref[...] = jnp.broadcast_to(cum[tm - 1:tm, :], carry_ref.shape)
    cout_ref[0] = carry_ref[...]
    _fox_kv_aug(fk_ref[...], fv_ref[...], cum, fka_ref, fva_ref)


def _cache_prep(lat, kr, fk, fv, logf128, wukv, *, batch, seq, tm):
    rows = batch * seq
    nt = seq // tm
    row_map = lambda b, i: (b * nt + i, 0)
    rows_spec = lambda width: pl.BlockSpec((tm, width), row_map)
    out_widths = [HEADS * MLA_DK, HEADS * MLA_V, HEADS * FOX_DK, HEADS * FOX_DK]
    return pl.pallas_call(
        _cache_kernel,
        grid=(batch, nt),
        in_specs=[rows_spec(MLA_KV_RANK), rows_spec(MLA_ROPE), rows_spec(HEADS * FOX_DIM),
                  rows_spec(HEADS * FOX_DIM), rows_spec(LANES),
                  pl.BlockSpec(wukv.shape, lambda b, i: (0, 0))],
        out_specs=[rows_spec(wd) for wd in out_widths]
                  + [pl.BlockSpec((1, 8, LANES), lambda b, i: (b, 0, 0))],
        out_shape=[jax.ShapeDtypeStruct((rows, wd), BF16) for wd in out_widths]
                  + [jax.ShapeDtypeStruct((batch, 8, LANES), F32)],
        scratch_shapes=[pltpu.VMEM((8, LANES), F32)],
        compiler_params=_params(("arbitrary", "arbitrary")),
        name="cache_prep",
    )(lat, kr, fk, fv, logf128, wukv)


def _attn_kernel(q_ref, k_ref, v_ref, g_ref, o_ref, *, tk, past, kv_len, chunk, scale, ones_col):
    tq = q_ref.shape[0]
    dv = v_ref.shape[1]
    q = q_ref[...]
    q0 = past + pl.program_id(2) * tq
    n_full = jnp.minimum((q0 // chunk + 1) * chunk, kv_len) // tk
    n_all = (jnp.minimum(((q0 + tq - 1) // chunk + 1) * chunk, kv_len) + tk - 1) // tk
    qpos = q0 + lax.broadcasted_iota(jnp.int32, (tq, 1), 0)
    key_limit = jnp.minimum((qpos // chunk + 1) * chunk, kv_len)
    kiota = lax.broadcasted_iota(jnp.int32, (tq, tk), 1)

    def step(j, carry, masked):
        m, l, acc = carry
        off = pl.multiple_of(j * tk, tk)
        k = k_ref[pl.ds(off, tk), :]
        v = v_ref[pl.ds(off, tk), :]
        s = lax.dot_general(q, k, (((1,), (1,)), ((), ())), preferred_element_type=F32)
        if scale is not None:
            s = s * scale
        if masked:
            s = jnp.where(kiota < key_limit - off, s, NEG_INF)
        m_new = jnp.maximum(m, jnp.max(s, axis=1, keepdims=True))
        alpha = jnp.exp(m - m_new)
        p = jnp.exp(s - m_new)
        acc = alpha * acc + jnp.dot(p.astype(BF16), v, preferred_element_type=F32)
        if not ones_col:
            l = alpha * l + jnp.sum(p, axis=1, keepdims=True)
        return m_new, l, acc

    carry = (jnp.full((tq, 1), NEG_INF, F32), jnp.zeros((tq, 1), F32), jnp.zeros((tq, dv), F32))
    carry = lax.fori_loop(0, n_full, functools.partial(step, masked=False), carry)
    _, l, acc = lax.fori_loop(n_full, n_all, functools.partial(step, masked=True), carry)
    if ones_col:
        l = acc[:, BIAS_LANE:BIAS_LANE + 1]
    o_ref[...] = (acc / l * g_ref[...]).astype(BF16)


def _attn(q, k, v, g, *, batch, tq_total, tk_total, tq, tk, dk, past, kv_len, chunk, scale, ones_col):
    nq = tq_total // tq
    dv = LANES
    kern = functools.partial(_attn_kernel, tk=tk, past=past, kv_len=kv_len, chunk=chunk,
                             scale=scale, ones_col=ones_col)
    return pl.pallas_call(
        kern,
        grid=(batch, HEADS, nq),
        in_specs=[pl.BlockSpec((tq, dk), lambda b, h, i: (b * nq + i, h)),
                  pl.BlockSpec((tk_total, dk), lambda b, h, i: (b, h)),
                  pl.BlockSpec((tk_total, dv), lambda b, h, i: (b, h)),
                  pl.BlockSpec((tq, dv), lambda b, h, i: (b * nq + i, h))],
        out_specs=pl.BlockSpec((tq, dv), lambda b, h, i: (b * nq + i, h)),
        out_shape=jax.ShapeDtypeStruct((batch * tq_total, HEADS * dv), BF16),
        compiler_params=_params(("arbitrary", "arbitrary", "arbitrary")),
        name="attn_mla" if scale is not None else "attn_fox",
    )(q, k, v, g)


def _out_kernel(mla_ref, fox_ref, mem_ref, x_ref, wmla_ref, wfox_ref, wmem_ref, g_ref, b_ref, y_ref):
    out = jnp.dot(mla_ref[...], wmla_ref[...], preferred_element_type=F32)
    out += jnp.dot(fox_ref[...], wfox_ref[...], preferred_element_type=F32)
    out += jnp.dot(mem_ref[...], wmem_ref[...], preferred_element_type=F32)
    z = DEEPNORM_ALPHA * x_ref[...] + out
    mu = jnp.mean(z, axis=-1, keepdims=True)
    zc = z - mu
    var = jnp.mean(zc * zc, axis=-1, keepdims=True)
    y_ref[...] = zc * lax.rsqrt(var + NORM_EPS) * g_ref[...] + b_ref[...]


def _out(mla, fox, mem, x, w, *, tm):
    rows = x.shape[0]
    rows_spec = lambda width: pl.BlockSpec((tm, width), lambda r: (r, 0))
    full_spec = lambda a: pl.BlockSpec(a.shape, lambda r: (0, 0))
    return pl.pallas_call(
        _out_kernel,
        grid=(rows // tm,),
        in_specs=[rows_spec(mla.shape[1]), rows_spec(fox.shape[1]), rows_spec(mem.shape[1]),
                  rows_spec(D_MODEL), full_spec(w['wo_mla']), full_spec(w['wo_fox']),
                  full_spec(w['wo_mem']), full_spec(w['ln_g']), full_spec(w['ln_b'])],
        out_specs=rows_spec(D_MODEL),
        out_shape=jax.ShapeDtypeStruct((rows, D_MODEL), F32),
        compiler_params=_params(("arbitrary",)),
        name="out_ln",
    )(mla, fox, mem, x, w['wo_mla'], w['wo_fox'], w['wo_mem'], w['ln_g'], w['ln_b'])


def _memkv_kernel(m_ref, w_ref, k_ref, v_ref):
    kv = jnp.dot(m_ref[...].astype(BF16), w_ref[...], preferred_element_type=F32)
    k_ref[...] = kv[:, :HEADS * MEM_DIM]
    v_ref[...] = kv[:, HEADS * MEM_DIM:]


def _mem_kv(mem, wmem):
    rows = mem.shape[0]
    width = HEADS * MEM_DIM
    return pl.pallas_call(
        _memkv_kernel,
        grid=(rows // N_MEM,),
        in_specs=[pl.BlockSpec((N_MEM, D_MODEL), lambda r: (r, 0)),
                  pl.BlockSpec(wmem.shape, lambda r: (0, 0))],
        out_specs=[pl.BlockSpec((N_MEM, width), lambda r: (r, 0))] * 2,
        out_shape=[jax.ShapeDtypeStruct((rows, width), F32)] * 2,
        compiler_params=_params(("arbitrary",)),
        name="mem_kv",
    )(mem, wmem)


def _rope_tables(pos):
    half = MLA_ROPE // 2
    inv = ROPE_THETA ** (-jnp.arange(half, dtype=F32) / half)
    ang = pos.astype(F32)[:, None] * inv[None, :]
    cos, sin = jnp.cos(ang), jnp.sin(ang)
    n = pos.shape[0]
    return (jnp.concatenate([cos, cos, jnp.ones((n, LANES - MLA_ROPE), F32)], axis=1),
            jnp.concatenate([-sin, sin, jnp.zeros((n, LANES - MLA_ROPE), F32)], axis=1))


def _layer_weights(w_in, b_f, qg, kvg, w_uq, w_ukv, w_out, ln_g, ln_b):
    c = [0]
    for n in (MLA_Q_RANK, MLA_KV_RANK, MLA_ROPE, HEADS * MLA_V, 256, 256, 256, HEADS, 256, 256, 256):
        c.append(c[-1] + n)
    s = lambda i: w_in[:, c[i]:c[i + 1]]
    win = jnp.concatenate([s(0), s(1), s(3), s(4), s(5), s(6), s(8), s(9), s(10), s(2), s(7),
                           jnp.zeros((D_MODEL, LANES - MLA_ROPE - HEADS), F32)], axis=1).astype(BF16)
    wuq = jnp.pad(w_uq.reshape(MLA_Q_RANK, HEADS, MLA_NOPE + MLA_ROPE),
                  ((0, 0), (0, 0), (0, MLA_DK - MLA_NOPE - MLA_ROPE))).reshape(MLA_Q_RANK, HEADS * MLA_DK)
    wo_fox = jnp.pad(w_out[HEADS * MLA_V:HEADS * MLA_V + HEADS * FOX_DIM].reshape(HEADS, FOX_DIM, D_MODEL),
                     ((0, 0), (0, FOX_DK - FOX_DIM), (0, 0))).reshape(HEADS * FOX_DK, D_MODEL)
    bf = jnp.zeros((1, LANES), F32).at[0, BIAS_LANE:BIAS_LANE + HEADS].set(b_f)
    return dict(win=win, wuq=wuq.astype(BF16), wukv=w_ukv.astype(BF16),
                qg=qg.reshape(1, -1), kvg=kvg.reshape(1, -1), bf=bf,
                wo_mla=w_out[:HEADS * MLA_V].astype(BF16), wo_fox=wo_fox.astype(BF16),
                wo_mem=w_out[HEADS * MLA_V + HEADS * FOX_DIM:].astype(BF16),
                ln_g=ln_g.reshape(1, -1), ln_b=ln_b.reshape(1, -1))


def _pad_rows(a, batch, seq, seq_pad):
    a = a.reshape(batch, seq, a.shape[-1])
    return jnp.pad(a, ((0, 0), (0, seq_pad - seq), (0, 0))).reshape(batch * seq_pad, a.shape[-1])


def kernel(x_prompt, x_sample, cache_mla_latent, cache_mla_krope, cache_fox_k, cache_fox_v, cache_fox_logf,
           cache_mem_k, cache_mem_v, mem_prompt, w_in, b_fox_f, mla_q_norm, mla_kv_norm, w_uq, w_ukv,
           w_mem_kv, w_out, ln_g, ln_b):
    batch, seq, _ = x_prompt.shape
    dec_batch, dec_seq, _ = x_sample.shape
    depth = w_in.shape[0]
    past = cache_mla_latent.shape[2]
    kv_len = past + dec_seq
    tk_s = 256
    kv_pad = -(-kv_len // tk_s) * tk_s

    cos_p, sin_p = _rope_tables(jnp.arange(seq))
    cos_s, sin_s = _rope_tables(past + jnp.arange(dec_seq))
    zero_carry = jnp.zeros((batch, 8, LANES), F32)

    yp = x_prompt.reshape(batch * seq, D_MODEL)
    ys = x_sample.reshape(dec_batch * dec_seq, D_MODEL)
    mem2d = mem_prompt.reshape(batch * N_MEM, D_MODEL)
    p_out, s_out, p_mk, p_mv = [], [], [], []
    for l in range(depth):
        w = _layer_weights(w_in[l], b_fox_f[l], mla_q_norm[l], mla_kv_norm[l], w_uq[l], w_ukv[l],
                           w_out[l], ln_g[l], ln_b[l])
        mk, mv = _mem_kv(mem2d, w_mem_kv[l].astype(BF16))
        (lat, kr, fk, fv, logf, q, k, v, fqa, fka, fva, gmla, gfox, mmem) = _proj(
            yp, zero_carry, mk.reshape(batch, N_MEM, -1), mv.reshape(batch, N_MEM, -1), cos_p, sin_p, w,
            batch=batch, seq=seq, tm=256)
        o_mla = _attn(q, k, v, gmla, batch=batch, tq_total=seq, tk_total=seq, tq=256, tk=512, dk=MLA_DK,
                      past=0, kv_len=seq, chunk=CHUNK, scale=MLA_SCALE, ones_col=False)
        o_fox = _attn(fqa, fka, fva, gfox, batch=batch, tq_total=seq, tk_total=seq, tq=256, tk=512, dk=FOX_DK,
                      past=0, kv_len=seq, chunk=1, scale=None, ones_col=True)
        yp = _out(o_mla, o_fox, mmem, yp, w, tm=512)
        p_out.append((lat, kr, fk, fv, logf))
        p_mk.append(mk)
        p_mv.append(mv)

        rows_c = dec_batch * past
        logf_c = jnp.pad(cache_fox_logf[l].reshape(rows_c, HEADS),
                         ((0, 0), (BIAS_LANE, LANES - BIAS_LANE - HEADS)))
        ck, cv, cfka, cfva, carry = _cache_prep(
            cache_mla_latent[l].reshape(rows_c, -1), cache_mla_krope[l].reshape(rows_c, -1),
            cache_fox_k[l].reshape(rows_c, -1), cache_fox_v[l].reshape(rows_c, -1), logf_c, w['wukv'],
            batch=dec_batch, seq=past, tm=512)
        (lat, kr, fk, fv, logf, q, k, v, fqa, fka, fva, gmla, gfox, mmem) = _proj(
            ys, carry, cache_mem_k[l].reshape(dec_batch, N_MEM, -1), cache_mem_v[l].reshape(dec_batch, N_MEM, -1),
            cos_s, sin_s, w, batch=dec_batch, seq=dec_seq, tm=dec_seq)

        def joined(cached, new):
            width = cached.shape[-1]
            a = jnp.concatenate([cached.reshape(dec_batch, past, width),
                                 new.reshape(dec_batch, dec_seq, width)], axis=1)
            return _pad_rows(a, dec_batch, kv_len, kv_pad)

        o_mla = _attn(q, joined(ck, k), joined(cv, v), gmla, batch=dec_batch, tq_total=dec_seq,
                      tk_total=kv_pad, tq=dec_seq, tk=tk_s, dk=MLA_DK, past=past, kv_len=kv_len,
                      chunk=CHUNK, scale=MLA_SCALE, ones_col=False)
        o_fox = _attn(fqa, joined(cfka, fka), joined(cfva, fva), gfox, batch=dec_batch, tq_total=dec_seq,
                      tk_total=kv_pad, tq=dec_seq, tk=tk_s, dk=FOX_DK, past=past, kv_len=kv_len,
                      chunk=1, scale=None, ones_col=True)
        ys = _out(o_mla, o_fox, mmem, ys, w, tm=dec_batch * dec_seq)
        s_out.append((lat, kr, fk, fv, logf))

    def stack(rows, i, shape):
        return jnp.stack([r[i].reshape(shape) for r in rows], axis=0)

    pb, sb = (batch, seq), (dec_batch, dec_seq)
    return (yp.reshape(batch, seq, D_MODEL), ys.reshape(dec_batch, dec_seq, D_MODEL),
            stack(p_out, 0, pb + (MLA_KV_RANK,)), stack(p_out, 1, pb + (MLA_ROPE,)),
            stack(p_out, 2, pb + (HEADS, FOX_DIM)), stack(p_out, 3, pb + (HEADS, FOX_DIM)),
            stack(p_out, 4, pb + (HEADS,)),
            jnp.stack([a.reshape(batch, N_MEM, HEADS, MEM_DIM) for a in p_mk], axis=0),
            jnp.stack([a.reshape(batch, N_MEM, HEADS, MEM_DIM) for a in p_mv], axis=0),
            stack(s_out, 0, sb + (MLA_KV_RANK,)), stack(s_out, 1, sb + (MLA_ROPE,)),
            stack(s_out, 2, sb + (HEADS, FOX_DIM)), stack(s_out, 3, sb + (HEADS, FOX_DIM)),
            stack(s_out, 4, sb + (HEADS,)))
```

```python
import functools

import jax
import jax.numpy as jnp
from jax import lax
from jax.experimental import pallas as pl
from jax.experimental.pallas import tpu as pltpu

F32 = jnp.float32
BF16 = jnp.bfloat16

D_MODEL = 1024
CHUNK = 64
N_MEM = 256
HEADS = 4
MLA_NOPE = 128
MLA_ROPE = 64
MLA_V = 128
MLA_Q_RANK = 384
MLA_KV_RANK = 256
FOX_DIM = 64
MEM_DIM = 64
ROPE_THETA = 10000.0
NORM_EPS = 1e-6
NEG_INF = -1e30
MLA_SCALE = (MLA_NOPE + MLA_ROPE) ** -0.5
FOX_SCALE = FOX_DIM ** -0.5
MEM_SCALE = MEM_DIM ** -0.5
DEPTH = 2
DEEPNORM_ALPHA = (2 * DEPTH) ** 0.25
LOG2E = 1.4426950408889634

LANES = 128
MLA_DK = 256
FOX_DK = 128
BIAS_LANE = 64
VMEM_LIMIT = 52 * 1024 * 1024

O_CQ, O_CKV, O_GMLA, O_FQ, O_FK, O_FV, O_GFOX, O_MQ, O_GMEM, O_KR, O_END = (
    0, 384, 640, 1152, 1408, 1664, 1920, 2176, 2432, 2688, 2816)


def _params(sem):
    return pltpu.CompilerParams(dimension_semantics=sem, vmem_limit_bytes=VMEM_LIMIT)


def _silu(x):
    return x * jax.nn.sigmoid(x)


def _rms(x, g):
    return x * lax.rsqrt(jnp.mean(x * x, axis=-1, keepdims=True) + NORM_EPS) * g


def _rope128(x, cos, sin):
    lane = lax.broadcasted_iota(jnp.int32, x.shape, 1)
    swapped = jnp.where(lane < MLA_ROPE // 2, pltpu.roll(x, LANES - MLA_ROPE // 2, 1),
                        pltpu.roll(x, MLA_ROPE // 2, 1))
    return x * cos + swapped * sin


def _cumsum_rows(x):
    n = x.shape[0]
    row = lax.broadcasted_iota(jnp.int32, x.shape, 0)
    shift = 1
    while shift < n:
        x = x + jnp.where(row >= shift, pltpu.roll(x, shift, 0), 0.0)
        shift *= 2
    return x


def _split3(x):
    hi = x.astype(BF16).astype(F32)
    r = x - hi
    mid = r.astype(BF16).astype(F32)
    return hi, mid, r - mid


def _head_cols(x, h):
    col = x[:, (h // 2) * LANES:(h // 2 + 1) * LANES]
    return pltpu.roll(col, FOX_DIM, 1) if h % 2 else col


def _fox_kv_aug(fk, fv, cum, fka_ref, fva_ref):
    rows = fk.shape[0]
    lane = lax.broadcasted_iota(jnp.int32, (rows, LANES), 1)
    for h in range(HEADS):
        c = jnp.broadcast_to(cum[:, BIAS_LANE + h:BIAS_LANE + h + 1], (rows, LANES)) * LOG2E
        hi, mid, lo = _split3(c)
        kb = jnp.where(lane < BIAS_LANE + 3, 1.0,
             jnp.where(lane == BIAS_LANE + 3, -hi,
             jnp.where(lane == BIAS_LANE + 4, -mid,
             jnp.where(lane == BIAS_LANE + 5, -lo, 0.0))))
        ka = jnp.where(lane < BIAS_LANE, _head_cols(fk, h), kb)
        va = jnp.where(lane < BIAS_LANE, _head_cols(fv, h),
                       jnp.where(lane == BIAS_LANE, 1.0, 0.0))
        fka_ref[:, h * LANES:(h + 1) * LANES] = ka.astype(BF16)
        fva_ref[:, h * LANES:(h + 1) * LANES] = va.astype(BF16)


def _mla_kv(lat, kr128, wukv_ref, k_ref, v_ref):
    rows = lat.shape[0]
    kv = jnp.dot(lat.astype(BF16), wukv_ref[...], preferred_element_type=F32)
    lane = lax.broadcasted_iota(jnp.int32, (rows, LANES), 1)
    krb = jnp.where(lane < MLA_ROPE, kr128, 0.0).astype(BF16)
    for h in range(HEADS):
        k_ref[:, h * MLA_DK:h * MLA_DK + MLA_NOPE] = kv[:, 2 * h * LANES:(2 * h + 1) * LANES].astype(BF16)
        k_ref[:, h * MLA_DK + MLA_NOPE:(h + 1) * MLA_DK] = krb
        v_ref[:, h * MLA_V:(h + 1) * MLA_V] = kv[:, (2 * h + 1) * LANES:(2 * h + 2) * LANES].astype(BF16)


def _proj_kernel(x_ref, win_ref, wuq_ref, wukv_ref, qg_ref, kvg_ref, bf_ref, cos_ref, sin_ref,
                 cin_ref, mk_ref, mv_ref,
                 lat_ref, kr_ref, fk_ref, fv_ref, logf_ref, q_ref, k_ref, v_ref,
                 fqa_ref, fka_ref, fva_ref, gmla_ref, gfox_ref, mmem_ref, carry_ref):
    tm = x_ref.shape[0]

    @pl.when(pl.program_id(1) == 0)
    def _():
        carry_ref[...] = cin_ref[0]

    xb = x_ref[...].astype(BF16)

    def seg(lo, hi):
        return jnp.dot(xb, win_ref[:, lo:hi], preferred_element_type=F32)

    cos = cos_ref[...]
    sin = sin_ref[...]
    lane = lax.broadcasted_iota(jnp.int32, (tm, LANES), 1)

    cqn = _rms(seg(O_CQ, O_CKV), qg_ref[...])
    q = jnp.dot(cqn.astype(BF16), wuq_ref[...], preferred_element_type=F32) * (MLA_SCALE * LOG2E)
    for h in range(HEADS):
        q_ref[:, h * MLA_DK:h * MLA_DK + MLA_NOPE] = q[:, 2 * h * LANES:(2 * h + 1) * LANES].astype(BF16)
        q_ref[:, h * MLA_DK + MLA_NOPE:(h + 1) * MLA_DK] = _rope128(
            q[:, (2 * h + 1) * LANES:(2 * h + 2) * LANES], cos, sin).astype(BF16)

    lat = _rms(seg(O_CKV, O_GMLA), kvg_ref[...])
    lat_ref[...] = lat
    krl = seg(O_KR, O_END)
    kr128 = _rope128(krl, cos, sin)
    kr_ref[...] = kr128[:, :MLA_ROPE]
    _mla_kv(lat, kr128, wukv_ref, k_ref, v_ref)

    z = krl + bf_ref[...]
    logf = jnp.minimum(z, 0.0) - jnp.log1p(jnp.exp(-jnp.abs(z)))
    logf = jnp.where((lane >= BIAS_LANE) & (lane < BIAS_LANE + HEADS), logf, 0.0)
    logf_ref[...] = pltpu.roll(logf, LANES - BIAS_LANE, 1)[:, :HEADS]
    cum = _cumsum_rows(logf) + carry_ref[0:1, :]
    carry_ref[...] = jnp.broadcast_to(cum[tm - 1:tm, :], carry_ref.shape)

    fq = seg(O_FQ, O_FK)
    fk = seg(O_FK, O_FV)
    fv = seg(O_FV, O_GFOX)
    fk_ref[...] = fk
    fv_ref[...] = fv
    _fox_kv_aug(fk, fv, cum, fka_ref, fva_ref)
    gfox = _silu(seg(O_GFOX, O_MQ))
    for h in range(HEADS):
        c = jnp.broadcast_to(cum[:, BIAS_LANE + h:BIAS_LANE + h + 1], (tm, LANES)) * LOG2E
        hi, mid, lo = _split3(c)
        qb = jnp.where(lane == BIAS_LANE, hi,
             jnp.where(lane == BIAS_LANE + 1, mid,
             jnp.where(lane == BIAS_LANE + 2, lo,
             jnp.where(lane < BIAS_LANE + 6, 1.0, 0.0))))
        qa = jnp.where(lane < BIAS_LANE, _head_cols(fq, h) * (FOX_SCALE * LOG2E), qb)
        fqa_ref[:, h * LANES:(h + 1) * LANES] = qa.astype(BF16)
        gfox_ref[:, h * LANES:(h + 1) * LANES] = jnp.where(lane < BIAS_LANE, _head_cols(gfox, h), 0.0)

    gmla_ref[...] = _silu(seg(O_GMLA, O_FQ))

    mq = seg(O_MQ, O_GMEM).astype(BF16)
    gmem = _silu(seg(O_GMEM, O_KR))
    mk = mk_ref[0]
    mv = mv_ref[0].astype(BF16)
    key_head = lax.broadcasted_iota(jnp.int32, mk.shape, 1) // MEM_DIM
    out_head = lax.broadcasted_iota(jnp.int32, (tm, HEADS * MEM_DIM), 1) // MEM_DIM
    o_mem = jnp.zeros((tm, HEADS * MEM_DIM), F32)
    for h in range(HEADS):
        mk_h = jnp.where(key_head == h, mk, 0.0).astype(BF16)
        s = lax.dot_general(mq, mk_h, (((1,), (1,)), ((), ())), preferred_element_type=F32) * MEM_SCALE
        e = jnp.exp(s - jnp.max(s, axis=-1, keepdims=True))
        p = e / jnp.sum(e, axis=-1, keepdims=True)
        o_h = jnp.dot(p.astype(BF16), mv, preferred_element_type=F32)
        o_mem = jnp.where(out_head == h, o_h, o_mem)
    mmem_ref[...] = (o_mem * gmem).astype(BF16)


def _proj(x, cin, mk, mv, cos, sin, w, *, batch, seq, tm):
    rows = batch * seq
    nt = seq // tm
    row_map = lambda b, i: (b * nt + i, 0)
    full = lambda b, i: (0, 0)
    per_b = lambda b, i: (b, 0, 0)
    tab_map = lambda b, i: (i, 0)

    def rows_spec(width):
        return pl.BlockSpec((tm, width), row_map)

    def full_spec(a):
        return pl.BlockSpec(a.shape, full)

    out_widths = [(MLA_KV_RANK, F32), (MLA_ROPE, F32), (HEADS * FOX_DIM, F32), (HEADS * FOX_DIM, F32),
                  (HEADS, F32), (HEADS * MLA_DK, BF16), (HEADS * MLA_DK, BF16), (HEADS * MLA_V, BF16),
                  (HEADS * FOX_DK, BF16), (HEADS * FOX_DK, BF16), (HEADS * FOX_DK, BF16),
                  (HEADS * MLA_V, F32), (HEADS * FOX_DK, F32), (HEADS * MEM_DIM, BF16)]
    return pl.pallas_call(
        _proj_kernel,
        grid=(batch, nt),
        in_specs=[rows_spec(D_MODEL), full_spec(w['win']), full_spec(w['wuq']), full_spec(w['wukv']),
                  full_spec(w['qg']), full_spec(w['kvg']), full_spec(w['bf']),
                  pl.BlockSpec((tm, LANES), tab_map), pl.BlockSpec((tm, LANES), tab_map),
                  pl.BlockSpec((1, 8, LANES), per_b),
                  pl.BlockSpec((1, N_MEM, HEADS * MEM_DIM), per_b),
                  pl.BlockSpec((1, N_MEM, HEADS * MEM_DIM), per_b)],
        out_specs=[rows_spec(wd) for wd, _ in out_widths],
        out_shape=[jax.ShapeDtypeStruct((rows, wd), dt) for wd, dt in out_widths],
        scratch_shapes=[pltpu.VMEM((8, LANES), F32)],
        compiler_params=_params(("arbitrary", "arbitrary")),
        name="proj",
    )(x, w['win'], w['wuq'], w['wukv'], w['qg'], w['kvg'], w['bf'], cos, sin, cin, mk, mv)


def _cache_kernel(lat_ref, kr_ref, fk_ref, fv_ref, logf_ref, wukv_ref,
                  k_ref, v_ref, fka_ref, fva_ref, cout_ref, carry_ref):
    tm = lat_ref.shape[0]

    @pl.when(pl.program_id(1) == 0)
    def _():
        carry_ref[...] = jnp.zeros_like(carry_ref)

    kr128 = jnp.concatenate([kr_ref[...], jnp.zeros((tm, LANES - MLA_ROPE), F32)], axis=1)
    _mla_kv(lat_ref[...], kr128, wukv_ref, k_ref, v_ref)
    cum = _cumsum_rows(logf_ref[...]) + carry_ref[0:1, :]
    carry_ref[...] = jnp.broadcast_to(cum[tm - 1:tm, :], carry_ref.shape)
    cout_ref[0] = carry_ref[...]
    _fox_kv_aug(fk_ref[...], fv_ref[...], cum, fka_ref, fva_ref)


def _cache_prep(lat, kr, fk, fv, logf128, wukv, *, batch, seq, tm):
    rows = batch * seq
    nt = seq // tm
    row_map = lambda b, i: (b * nt + i, 0)
    rows_spec = lambda width: pl.BlockSpec((tm, width), row_map)
    out_widths = [HEADS * MLA_DK, HEADS * MLA_V, HEADS * FOX_DK, HEADS * FOX_DK]
    return pl.pallas_call(
        _cache_kernel,
        grid=(batch, nt),
        in_specs=[rows_spec(MLA_KV_RANK), rows_spec(MLA_ROPE), rows_spec(HEADS * FOX_DIM),
                  rows_spec(HEADS * FOX_DIM), rows_spec(LANES),
                  pl.BlockSpec(wukv.shape, lambda b, i: (0, 0))],
        out_specs=[rows_spec(wd) for wd in out_widths]
                  + [pl.BlockSpec((1, 8, LANES), lambda b, i: (b, 0, 0))],
        out_shape=[jax.ShapeDtypeStruct((rows, wd), BF16) for wd in out_widths]
                  + [jax.ShapeDtypeStruct((batch, 8, LANES), F32)],
        scratch_shapes=[pltpu.VMEM((8, LANES), F32)],
        compiler_params=_params(("arbitrary", "arbitrary")),
        name="cache_prep",
    )(lat, kr, fk, fv, logf128, wukv)


def _attn_kernel(q_ref, k_ref, v_ref, g_ref, o_ref, *, tk, dk, past, kv_len, chunk, ones_col):
    tq = q_ref.shape[0]
    dv = LANES
    q0 = past + pl.program_id(1) * tq
    n_full = jnp.minimum((q0 // chunk + 1) * chunk, kv_len) // tk
    n_all = (jnp.minimum(((q0 + tq - 1) // chunk + 1) * chunk, kv_len) + tk - 1) // tk
    qpos = q0 + lax.broadcasted_iota(jnp.int32, (tq, 1), 0)
    key_limit = jnp.minimum((qpos // chunk + 1) * chunk, kv_len)
    kiota = lax.broadcasted_iota(jnp.int32, (tq, tk), 1)

    def step(j, carry, masked):
        off = pl.multiple_of(j * tk, tk)
        out = []
        for h in range(HEADS):
            m, l, acc = carry[3 * h:3 * h + 3]
            q = q_ref[:, h * dk:(h + 1) * dk]
            k = k_ref[pl.ds(off, tk), h * dk:(h + 1) * dk]
            v = v_ref[pl.ds(off, tk), h * dv:(h + 1) * dv]
            s = lax.dot_general(q, k, (((1,), (1,)), ((), ())), preferred_element_type=F32)
            if masked:
                s = jnp.where(kiota < key_limit - off, s, NEG_INF)
            m_new = jnp.maximum(m, jnp.max(s, axis=1, keepdims=True))
            alpha = jnp.exp2(m - m_new)
            p = jnp.exp2(s - m_new)
            acc = alpha * acc + jnp.dot(p.astype(BF16), v, preferred_element_type=F32)
            if not ones_col:
                l = alpha * l + jnp.sum(p, axis=1, keepdims=True)
            out += [m_new, l, acc]
        return tuple(out)

    carry = (jnp.full((tq, 1), NEG_INF, F32), jnp.zeros((tq, 1), F32), jnp.zeros((tq, dv), F32)) * HEADS
    carry = lax.fori_loop(0, n_full, functools.partial(step, masked=False), carry)
    carry = lax.fori_loop(n_full, n_all, functools.partial(step, masked=True), carry)
    for h in range(HEADS):
        _, l, acc = carry[3 * h:3 * h + 3]
        if ones_col:
            l = acc[:, BIAS_LANE:BIAS_LANE + 1]
        o_ref[:, h * dv:(h + 1) * dv] = (acc / l * g_ref[:, h * dv:(h + 1) * dv]).astype(BF16)


def _attn(q, k, v, g, *, name, batch, tq_total, tk_total, tq, tk, dk, past, kv_len, chunk, ones_col):
    nq = tq_total // tq
    dv = LANES
    kern = functools.partial(_attn_kernel, tk=tk, dk=dk, past=past, kv_len=kv_len, chunk=chunk,
                             ones_col=ones_col)
    return pl.pallas_call(
        kern,
        grid=(batch, nq),
        in_specs=[pl.BlockSpec((tq, HEADS * dk), lambda b, i: (b * nq + i, 0)),
                  pl.BlockSpec((tk_total, HEADS * dk), lambda b, i: (b, 0)),
                  pl.BlockSpec((tk_total, HEADS * dv), lambda b, i: (b, 0)),
                  pl.BlockSpec((tq, HEADS * dv), lambda b, i: (b * nq + i, 0))],
        out_specs=pl.BlockSpec((tq, HEADS * dv), lambda b, i: (b * nq + i, 0)),
        out_shape=jax.ShapeDtypeStruct((batch * tq_total, HEADS * dv), BF16),
        compiler_params=_params(("arbitrary", "arbitrary")),
        name=name,
    )(q, k, v, g)


def _out_kernel(mla_ref, fox_ref, mem_ref, x_ref, wmla_ref, wfox_ref, wmem_ref, g_ref, b_ref, y_ref):
    out = jnp.dot(mla_ref[...], wmla_ref[...], preferred_element_type=F32)
    out += jnp.dot(fox_ref[...], wfox_ref[...], preferred_element_type=F32)
    out += jnp.dot(mem_ref[...], wmem_ref[...], preferred_element_type=F32)
    z = DEEPNORM_ALPHA * x_ref[...] + out
    mu = jnp.mean(z, axis=-1, keepdims=True)
    zc = z - mu
    var = jnp.mean(zc * zc, axis=-1, keepdims=True)
    y_ref[...] = zc * lax.rsqrt(var + NORM_EPS) * g_ref[...] + b_ref[...]


def _out(mla, fox, mem, x, w, *, tm):
    rows = x.shape[0]
    rows_spec = lambda width: pl.BlockSpec((tm, width), lambda r: (r, 0))
    full_spec = lambda a: pl.BlockSpec(a.shape, lambda r: (0, 0))
    return pl.pallas_call(
        _out_kernel,
        grid=(rows // tm,),
        in_specs=[rows_spec(mla.shape[1]), rows_spec(fox.shape[1]), rows_spec(mem.shape[1]),
                  rows_spec(D_MODEL), full_spec(w['wo_mla']), full_spec(w['wo_fox']),
                  full_spec(w['wo_mem']), full_spec(w['ln_g']), full_spec(w['ln_b'])],
        out_specs=rows_spec(D_MODEL),
        out_shape=jax.ShapeDtypeStruct((rows, D_MODEL), F32),
        compiler_params=_params(("arbitrary",)),
        name="out_ln",
    )(mla, fox, mem, x, w['wo_mla'], w['wo_fox'], w['wo_mem'], w['ln_g'], w['ln_b'])


def _memkv_kernel(m_ref, w_ref, k_ref, v_ref):
    kv = jnp.dot(m_ref[...].astype(BF16), w_ref[...], preferred_element_type=F32)
    k_ref[...] = kv[:, :HEADS * MEM_DIM]
    v_ref[...] = kv[:, HEADS * MEM_DIM:]


def _mem_kv(mem, wmem):
    rows = mem.shape[0]
    width = HEADS * MEM_DIM
    return pl.pallas_call(
        _memkv_kernel,
        grid=(rows // N_MEM,),
        in_specs=[pl.BlockSpec((N_MEM, D_MODEL), lambda r: (r, 0)),
                  pl.BlockSpec(wmem.shape, lambda r: (0, 0))],
        out_specs=[pl.BlockSpec((N_MEM, width), lambda r: (r, 0))] * 2,
        out_shape=[jax.ShapeDtypeStruct((rows, width), F32)] * 2,
        compiler_params=_params(("arbitrary",)),
        name="mem_kv",
    )(mem, wmem)


def _rope_tables(pos):
    half = MLA_ROPE // 2
    inv = ROPE_THETA ** (-jnp.arange(half, dtype=F32) / half)
    ang = pos.astype(F32)[:, None] * inv[None, :]
    cos, sin = jnp.cos(ang), jnp.sin(ang)
    n = pos.shape[0]
    return (jnp.concatenate([cos, cos, jnp.ones((n, LANES - MLA_ROPE), F32)], axis=1),
            jnp.concatenate([-sin, sin, jnp.zeros((n, LANES - MLA_ROPE), F32)], axis=1))


def _layer_weights(w_in, b_f, qg, kvg, w_uq, w_ukv, w_out, ln_g, ln_b):
    c = [0]
    for n in (MLA_Q_RANK, MLA_KV_RANK, MLA_ROPE, HEADS * MLA_V, 256, 256, 256, HEADS, 256, 256, 256):
        c.append(c[-1] + n)
    s = lambda i: w_in[:, c[i]:c[i + 1]]
    win = jnp.concatenate([s(0), s(1), s(3), s(4), s(5), s(6), s(8), s(9), s(10), s(2), s(7),
                           jnp.zeros((D_MODEL, LANES - MLA_ROPE - HEADS), F32)], axis=1).astype(BF16)
    wuq = jnp.pad(w_uq.reshape(MLA_Q_RANK, HEADS, MLA_NOPE + MLA_ROPE),
                  ((0, 0), (0, 0), (0, MLA_DK - MLA_NOPE - MLA_ROPE))).reshape(MLA_Q_RANK, HEADS * MLA_DK)
    wo_fox = jnp.pad(w_out[HEADS * MLA_V:HEADS * MLA_V + HEADS * FOX_DIM].reshape(HEADS, FOX_DIM, D_MODEL),
                     ((0, 0), (0, FOX_DK - FOX_DIM), (0, 0))).reshape(HEADS * FOX_DK, D_MODEL)
    bf = jnp.zeros((1, LANES), F32).at[0, BIAS_LANE:BIAS_LANE + HEADS].set(b_f)
    return dict(win=win, wuq=wuq.astype(BF16), wukv=w_ukv.astype(BF16),
                qg=qg.reshape(1, -1), kvg=kvg.reshape(1, -1), bf=bf,
                wo_mla=w_out[:HEADS * MLA_V].astype(BF16), wo_fox=wo_fox.astype(BF16),
                wo_mem=w_out[HEADS * MLA_V + HEADS * FOX_DIM:].astype(BF16),
                ln_g=ln_g.reshape(1, -1), ln_b=ln_b.reshape(1, -1))


def _pad_rows(a, batch, seq, seq_pad):
    a = a.reshape(batch, seq, a.shape[-1])
    return jnp.pad(a, ((0, 0), (0, seq_pad - seq), (0, 0))).reshape(batch * seq_pad, a.shape[-1])


def kernel(x_prompt, x_sample, cache_mla_latent, cache_mla_krope, cache_fox_k, cache_fox_v, cache_fox_logf,
           cache_mem_k, cache_mem_v, mem_prompt, w_in, b_fox_f, mla_q_norm, mla_kv_norm, w_uq, w_ukv,
           w_mem_kv, w_out, ln_g, ln_b):
    batch, seq, _ = x_prompt.shape
    dec_batch, dec_seq, _ = x_sample.shape
    depth = w_in.shape[0]
    past = cache_mla_latent.shape[2]
    kv_len = past + dec_seq
    tk_s = 768
    kv_pad = -(-kv_len // tk_s) * tk_s

    cos_p, sin_p = _rope_tables(jnp.arange(seq))
    cos_s, sin_s = _rope_tables(past + jnp.arange(dec_seq))
    zero_carry = jnp.zeros((batch, 8, LANES), F32)

    yp = x_prompt.reshape(batch * seq, D_MODEL)
    ys = x_sample.reshape(dec_batch * dec_seq, D_MODEL)
    mem2d = mem_prompt.reshape(batch * N_MEM, D_MODEL)
    p_out, s_out, p_mk, p_mv = [], [], [], []
    for l in range(depth):
        w = _layer_weights(w_in[l], b_fox_f[l], mla_q_norm[l], mla_kv_norm[l], w_uq[l], w_ukv[l],
                           w_out[l], ln_g[l], ln_b[l])
        mk, mv = _mem_kv(mem2d, w_mem_kv[l].astype(BF16))
        (lat, kr, fk, fv, logf, q, k, v, fqa, fka, fva, gmla, gfox, mmem) = _proj(
            yp, zero_carry, mk.reshape(batch, N_MEM, -1), mv.reshape(batch, N_MEM, -1), cos_p, sin_p, w,
            batch=batch, seq=seq, tm=256)
        o_mla = _attn(q, k, v, gmla, name="attn_mla", batch=batch, tq_total=seq, tk_total=seq, tq=512, tk=512,
                      dk=MLA_DK, past=0, kv_len=seq, chunk=CHUNK, ones_col=False)
        o_fox = _attn(fqa, fka, fva, gfox, name="attn_fox", batch=batch, tq_total=seq, tk_total=seq, tq=512,
                      tk=512, dk=FOX_DK, past=0, kv_len=seq, chunk=1, ones_col=True)
        yp = _out(o_mla, o_fox, mmem, yp, w, tm=512)
        p_out.append((lat, kr, fk, fv, logf))
        p_mk.append(mk)
        p_mv.append(mv)

        rows_c = dec_batch * past
        logf_c = jnp.pad(cache_fox_logf[l].reshape(rows_c, HEADS),
                         ((0, 0), (BIAS_LANE, LANES - BIAS_LANE - HEADS)))
        ck, cv, cfka, cfva, carry = _cache_prep(
            cache_mla_latent[l].reshape(rows_c, -1), cache_mla_krope[l].reshape(rows_c, -1),
            cache_fox_k[l].reshape(rows_c, -1), cache_fox_v[l].reshape(rows_c, -1), logf_c, w['wukv'],
            batch=dec_batch, seq=past, tm=512)
        (lat, kr, fk, fv, logf, q, k, v, fqa, fka, fva, gmla, gfox, mmem) = _proj(
            ys, carry, cache_mem_k[l].reshape(dec_batch, N_MEM, -1), cache_mem_v[l].reshape(dec_batch, N_MEM, -1),
            cos_s, sin_s, w, batch=dec_batch, seq=dec_seq, tm=dec_seq)

        def joined(cached, new):
            width = cached.shape[-1]
            a = jnp.concatenate([cached.reshape(dec_batch, past, width),
                                 new.reshape(dec_batch, dec_seq, width)], axis=1)
            return _pad_rows(a, dec_batch, kv_len, kv_pad)

        o_mla = _attn(q, joined(ck, k), joined(cv, v), gmla, name="attn_mla_s", batch=dec_batch,
                      tq_total=dec_seq, tk_total=kv_pad, tq=dec_seq, tk=tk_s, dk=MLA_DK, past=past,
                      kv_len=kv_len, chunk=CHUNK, ones_col=False)
        o_fox = _attn(fqa, joined(cfka, fka), joined(cfva, fva), gfox, name="attn_fox_s", batch=dec_batch,
                      tq_total=dec_seq, tk_total=kv_pad, tq=dec_seq, tk=tk_s, dk=FOX_DK, past=past,
                      kv_len=kv_len, chunk=1, ones_col=True)
        ys = _out(o_mla, o_fox, mmem, ys, w, tm=dec_batch * dec_seq)
        s_out.append((lat, kr, fk, fv, logf))

    def stack(rows, i, shape):
        return jnp.stack([r[i].reshape(shape) for r in rows], axis=0)

    pb, sb = (batch, seq), (dec_batch, dec_seq)
    return (yp.reshape(batch, seq, D_MODEL), ys.reshape(dec_batch, dec_seq, D_MODEL),
            stack(p_out, 0, pb + (MLA_KV_RANK,)), stack(p_out, 1, pb + (MLA_ROPE,)),
            stack(p_out, 2, pb + (HEADS, FOX_DIM)), stack(p_out, 3, pb + (HEADS, FOX_DIM)),
            stack(p_out, 4, pb + (HEADS,)),
            jnp.stack([a.reshape(batch, N_MEM, HEADS, MEM_DIM) for a in p_mk], axis=0),
            jnp.stack([a.reshape(batch, N_MEM, HEADS, MEM_DIM) for a in p_mv], axis=0),
            stack(s_out, 0, sb + (MLA_KV_RANK,)), stack(s_out, 1, sb + (MLA_ROPE,)),
            stack(s_out, 2, sb + (HEADS, FOX_DIM)), stack(s_out, 3, sb + (HEADS, FOX_DIM)),
            stack(s_out, 4, sb + (HEADS,)))
```

```python
import functools

import jax
import jax.numpy as jnp
from jax import lax
from jax.experimental import pallas as pl
from jax.experimental.pallas import tpu as pltpu

F32 = jnp.float32
BF16 = jnp.bfloat16

D_MODEL = 1024
CHUNK = 64
N_MEM = 256
HEADS = 4
MLA_NOPE = 128
MLA_ROPE = 64
MLA_V = 128
MLA_Q_RANK = 384
MLA_KV_RANK = 256
FOX_DIM = 64
MEM_DIM = 64
ROPE_THETA = 10000.0
NORM_EPS = 1e-6
NEG_INF = -1e30
MLA_SCALE = (MLA_NOPE + MLA_ROPE) ** -0.5
FOX_SCALE = FOX_DIM ** -0.5
MEM_SCALE = MEM_DIM ** -0.5
DEPTH = 2
DEEPNORM_ALPHA = (2 * DEPTH) ** 0.25
LOG2E = 1.4426950408889634

LANES = 128
MLA_DK = 256
FOX_DK = 128
BIAS_LANE = 64
VMEM_LIMIT = 52 * 1024 * 1024
ATTN_ROW_BLOCK = 32

O_CQ, O_CKV, O_GMLA, O_FQ, O_FK, O_FV, O_GFOX, O_MQ, O_GMEM, O_KR, O_END = (
    0, 384, 640, 1152, 1408, 1664, 1920, 2176, 2432, 2688, 2816)


def _params(sem):
    return pltpu.CompilerParams(dimension_semantics=sem, vmem_limit_bytes=VMEM_LIMIT)


def _silu(x):
    return x * jax.nn.sigmoid(x)


def _rms(x, g):
    return x * lax.rsqrt(jnp.mean(x * x, axis=-1, keepdims=True) + NORM_EPS) * g


def _rope128(x, cos, sin):
    lane = lax.broadcasted_iota(jnp.int32, x.shape, 1)
    swapped = jnp.where(lane < MLA_ROPE // 2, pltpu.roll(x, LANES - MLA_ROPE // 2, 1),
                        pltpu.roll(x, MLA_ROPE // 2, 1))
    return x * cos + swapped * sin


def _cumsum_rows(x):
    n = x.shape[0]
    row = lax.broadcasted_iota(jnp.int32, x.shape, 0)
    shift = 1
    while shift < n:
        x = x + jnp.where(row >= shift, pltpu.roll(x, shift, 0), 0.0)
        shift *= 2
    return x


def _split3(x):
    hi = x.astype(BF16).astype(F32)
    r = x - hi
    mid = r.astype(BF16).astype(F32)
    return hi, mid, r - mid


def _head_cols(x, h):
    col = x[:, (h // 2) * LANES:(h // 2 + 1) * LANES]
    return pltpu.roll(col, FOX_DIM, 1) if h % 2 else col


def _fox_kv_aug(fk, fv, cum, fka_ref, fva_ref):
    rows = fk.shape[0]
    lane = lax.broadcasted_iota(jnp.int32, (rows, LANES), 1)
    for h in range(HEADS):
        c = jnp.broadcast_to(cum[:, BIAS_LANE + h:BIAS_LANE + h + 1], (rows, LANES)) * LOG2E
        hi, mid, lo = _split3(c)
        kb = jnp.where(lane < BIAS_LANE + 3, 1.0,
             jnp.where(lane == BIAS_LANE + 3, -hi,
             jnp.where(lane == BIAS_LANE + 4, -mid,
             jnp.where(lane == BIAS_LANE + 5, -lo, 0.0))))
        ka = jnp.where(lane < BIAS_LANE, _head_cols(fk, h), kb)
        va = jnp.where(lane < BIAS_LANE, _head_cols(fv, h),
                       jnp.where(lane == BIAS_LANE, 1.0, 0.0))
        fka_ref[:, h * LANES:(h + 1) * LANES] = ka.astype(BF16)
        fva_ref[:, h * LANES:(h + 1) * LANES] = va.astype(BF16)


def _mla_kv(lat, kr128, wukv_ref, k_ref, v_ref):
    rows = lat.shape[0]
    kv = jnp.dot(lat.astype(BF16), wukv_ref[...], preferred_element_type=F32)
    lane = lax.broadcasted_iota(jnp.int32, (rows, LANES), 1)
    krb = jnp.where(lane < MLA_ROPE, kr128, 0.0).astype(BF16)
    for h in range(HEADS):
        k_ref[:, h * MLA_DK:h * MLA_DK + MLA_NOPE] = kv[:, 2 * h * LANES:(2 * h + 1) * LANES].astype(BF16)
        k_ref[:, h * MLA_DK + MLA_NOPE:(h + 1) * MLA_DK] = krb
        v_ref[:, h * MLA_V:(h + 1) * MLA_V] = kv[:, (2 * h + 1) * LANES:(2 * h + 2) * LANES].astype(BF16)


def _proj_kernel(x_ref, win_ref, wuq_ref, wukv_ref, qg_ref, kvg_ref, bf_ref, cos_ref, sin_ref,
                 cin_ref, mk_ref, mv_ref,
                 lat_ref, kr_ref, fk_ref, fv_ref, logf_ref, q_ref, k_ref, v_ref,
                 fqa_ref, fka_ref, fva_ref, gmla_ref, gfox_ref, mmem_ref, carry_ref):
    tm = x_ref.shape[0]

    @pl.when(pl.program_id(1) == 0)
    def _():
        carry_ref[...] = cin_ref[0]

    xb = x_ref[...].astype(BF16)

    def seg(lo, hi):
        return jnp.dot(xb, win_ref[:, lo:hi], preferred_element_type=F32)

    cos = cos_ref[...]
    sin = sin_ref[...]
    lane = lax.broadcasted_iota(jnp.int32, (tm, LANES), 1)

    cqn = _rms(seg(O_CQ, O_CKV), qg_ref[...])
    q = jnp.dot(cqn.astype(BF16), wuq_ref[...], preferred_element_type=F32) * (MLA_SCALE * LOG2E)
    for h in range(HEADS):
        q_ref[:, h * MLA_DK:h * MLA_DK + MLA_NOPE] = q[:, 2 * h * LANES:(2 * h + 1) * LANES].astype(BF16)
        q_ref[:, h * MLA_DK + MLA_NOPE:(h + 1) * MLA_DK] = _rope128(
            q[:, (2 * h + 1) * LANES:(2 * h + 2) * LANES], cos, sin).astype(BF16)

    lat = _rms(seg(O_CKV, O_GMLA), kvg_ref[...])
    lat_ref[...] = lat
    krl = seg(O_KR, O_END)
    kr128 = _rope128(krl, cos, sin)
    kr_ref[...] = kr128[:, :MLA_ROPE]
    _mla_kv(lat, kr128, wukv_ref, k_ref, v_ref)

    z = krl + bf_ref[...]
    logf = jnp.minimum(z, 0.0) - jnp.log1p(jnp.exp(-jnp.abs(z)))
    logf = jnp.where((lane >= BIAS_LANE) & (lane < BIAS_LANE + HEADS), logf, 0.0)
    logf_ref[...] = pltpu.roll(logf, LANES - BIAS_LANE, 1)[:, :HEADS]
    cum = _cumsum_rows(logf) + carry_ref[0:1, :]
    carry_ref[...] = jnp.broadcast_to(cum[tm - 1:tm, :], carry_ref.shape)

    fq = seg(O_FQ, O_FK)
    fk = seg(O_FK, O_FV)
    fv = seg(O_FV, O_GFOX)
    fk_ref[...] = fk
    fv_ref[...] = fv
    _fox_kv_aug(fk, fv, cum, fka_ref, fva_ref)
    gfox = _silu(seg(O_GFOX, O_MQ))
    for h in range(HEADS):
        c = jnp.broadcast_to(cum[:, BIAS_LANE + h:BIAS_LANE + h + 1], (tm, LANES)) * LOG2E
        hi, mid, lo = _split3(c)
        qb = jnp.where(lane == BIAS_LANE, hi,
             jnp.where(lane == BIAS_LANE + 1, mid,
             jnp.where(lane == BIAS_LANE + 2, lo,
             jnp.where(lane < BIAS_LANE + 6, 1.0, 0.0))))
        qa = jnp.where(lane < BIAS_LANE, _head_cols(fq, h) * (FOX_SCALE * LOG2E), qb)
        fqa_ref[:, h * LANES:(h + 1) * LANES] = qa.astype(BF16)
        gfox_ref[:, h * LANES:(h + 1) * LANES] = jnp.where(lane < BIAS_LANE, _head_cols(gfox, h), 0.0)

    gmla_ref[...] = _silu(seg(O_GMLA, O_FQ))

    mq = seg(O_MQ, O_GMEM).astype(BF16)
    gmem = _silu(seg(O_GMEM, O_KR))
    mk = mk_ref[0]
    mv = mv_ref[0].astype(BF16)
    key_head = lax.broadcasted_iota(jnp.int32, mk.shape, 1) // MEM_DIM
    out_head = lax.broadcasted_iota(jnp.int32, (tm, HEADS * MEM_DIM), 1) // MEM_DIM
    o_mem = jnp.zeros((tm, HEADS * MEM_DIM), F32)
    for h in range(HEADS):
        mk_h = jnp.where(key_head == h, mk, 0.0).astype(BF16)
        s = lax.dot_general(mq, mk_h, (((1,), (1,)), ((), ())), preferred_element_type=F32) * MEM_SCALE
        e = jnp.exp(s - jnp.max(s, axis=-1, keepdims=True))
        p = e / jnp.sum(e, axis=-1, keepdims=True)
        o_h = jnp.dot(p.astype(BF16), mv, preferred_element_type=F32)
        o_mem = jnp.where(out_head == h, o_h, o_mem)
    mmem_ref[...] = (o_mem * gmem).astype(BF16)


def _proj(x, cin, mk, mv, cos, sin, w, *, batch, seq, tm):
    rows = batch * seq
    nt = seq // tm
    row_map = lambda b, i: (b * nt + i, 0)
    full = lambda b, i: (0, 0)
    per_b = lambda b, i: (b, 0, 0)
    tab_map = lambda b, i: (i, 0)

    def rows_spec(width):
        return pl.BlockSpec((tm, width), row_map)

    def full_spec(a):
        return pl.BlockSpec(a.shape, full)

    out_widths = [(MLA_KV_RANK, F32), (MLA_ROPE, F32), (HEADS * FOX_DIM, F32), (HEADS * FOX_DIM, F32),
                  (HEADS, F32), (HEADS * MLA_DK, BF16), (HEADS * MLA_DK, BF16), (HEADS * MLA_V, BF16),
                  (HEADS * FOX_DK, BF16), (HEADS * FOX_DK, BF16), (HEADS * FOX_DK, BF16),
                  (HEADS * MLA_V, F32), (HEADS * FOX_DK, F32), (HEADS * MEM_DIM, BF16)]
    return pl.pallas_call(
        _proj_kernel,
        grid=(batch, nt),
        in_specs=[rows_spec(D_MODEL), full_spec(w['win']), full_spec(w['wuq']), full_spec(w['wukv']),
                  full_spec(w['qg']), full_spec(w['kvg']), full_spec(w['bf']),
                  pl.BlockSpec((tm, LANES), tab_map), pl.BlockSpec((tm, LANES), tab_map),
                  pl.BlockSpec((1, 8, LANES), per_b),
                  pl.BlockSpec((1, N_MEM, HEADS * MEM_DIM), per_b),
                  pl.BlockSpec((1, N_MEM, HEADS * MEM_DIM), per_b)],
        out_specs=[rows_spec(wd) for wd, _ in out_widths],
        out_shape=[jax.ShapeDtypeStruct((rows, wd), dt) for wd, dt in out_widths],
        scratch_shapes=[pltpu.VMEM((8, LANES), F32)],
        compiler_params=_params(("arbitrary", "arbitrary")),
        name="proj",
    )(x, w['win'], w['wuq'], w['wukv'], w['qg'], w['kvg'], w['bf'], cos, sin, cin, mk, mv)


def _cache_kernel(lat_ref, kr_ref, fk_ref, fv_ref, logf_ref, wukv_ref,
                  k_ref, v_ref, fka_ref, fva_ref, cout_ref, carry_ref):
    tm = lat_ref.shape[0]

    @pl.when(pl.program_id(1) == 0)
    def _():
        carry_ref[...] = jnp.zeros_like(carry_ref)

    kr128 = jnp.concatenate([kr_ref[...], jnp.zeros((tm, LANES - MLA_ROPE), F32)], axis=1)
    _mla_kv(lat_ref[...], kr128, wukv_ref, k_ref, v_ref)
    cum = _cumsum_rows(logf_ref[...]) + carry_ref[0:1, :]
    carry_ref[...] = jnp.broadcast_to(cum[tm - 1:tm, :], carry_ref.shape)
    cout_ref[0] = carry_ref[...]
    _fox_kv_aug(fk_ref[...], fv_ref[...], cum, fka_ref, fva_ref)


def _cache_prep(lat, kr, fk, fv, logf128, wukv, *, batch, seq, tm):
    rows = batch * seq
    nt = seq // tm
    row_map = lambda b, i: (b * nt + i, 0)
    rows_spec = lambda width: pl.BlockSpec((tm, width), row_map)
    out_widths = [HEADS * MLA_DK, HEADS * MLA_V, HEADS * FOX_DK, HEADS * FOX_DK]
    return pl.pallas_call(
        _cache_kernel,
        grid=(batch, nt),
        in_specs=[rows_spec(MLA_KV_RANK), rows_spec(MLA_ROPE), rows_spec(HEADS * FOX_DIM),
                  rows_spec(HEADS * FOX_DIM), rows_spec(LANES),
                  pl.BlockSpec(wukv.shape, lambda b, i: (0, 0))],
        out_specs=[rows_spec(wd) for wd in out_widths]
                  + [pl.BlockSpec((1, 8, LANES), lambda b, i: (b, 0, 0))],
        out_shape=[jax.ShapeDtypeStruct((rows, wd), BF16) for wd in out_widths]
                  + [jax.ShapeDtypeStruct((batch, 8, LANES), F32)],
        scratch_shapes=[pltpu.VMEM((8, LANES), F32)],
        compiler_params=_params(("arbitrary", "arbitrary")),
        name="cache_prep",
    )(lat, kr, fk, fv, logf128, wukv)


def _attn_kernel(q_ref, k_ref, v_ref, g_ref, o_ref, s_ref, p_ref, m_ref, l_ref, acc_ref,
                 *, tk, dk, past, kv_len, chunk, ones_col):
    tq = q_ref.shape[0]
    dv = LANES
    rb = min(ATTN_ROW_BLOCK, tq)
    q0 = past + pl.program_id(1) * tq
    n_full = jnp.minimum((q0 // chunk + 1) * chunk, kv_len) // tk
    n_all = (jnp.minimum(((q0 + tq - 1) // chunk + 1) * chunk, kv_len) + tk - 1) // tk
    lane = lax.broadcasted_iota(jnp.int32, (rb, LANES), 1)

    m_ref[...] = jnp.full(m_ref.shape, NEG_INF, F32)
    l_ref[...] = jnp.zeros(l_ref.shape, F32)
    acc_ref[...] = jnp.zeros(acc_ref.shape, F32)

    def step(j, carry, masked):
        off = pl.multiple_of(j * tk, tk)
        for h in range(HEADS):
            s_ref[h] = lax.dot_general(q_ref[:, h * dk:(h + 1) * dk], k_ref[pl.ds(off, tk), h * dk:(h + 1) * dk],
                                       (((1,), (1,)), ((), ())), preferred_element_type=F32)
        for h in range(HEADS):
            hs = slice(h * dv, (h + 1) * dv)
            for r in range(0, tq, rb):
                rows = slice(r, r + rb)
                cols = [s_ref[h, rows, c * LANES:(c + 1) * LANES] for c in range(tk // LANES)]
                if masked:
                    qpos = q0 + r + lax.broadcasted_iota(jnp.int32, (rb, LANES), 0)
                    limit = jnp.minimum((qpos // chunk + 1) * chunk, kv_len) - off
                    cols = [jnp.where(lane + c * LANES < limit, x, NEG_INF) for c, x in enumerate(cols)]
                smax = functools.reduce(jnp.maximum, cols)
                m_old = m_ref[rows, hs]
                m_new = jnp.maximum(m_old, jnp.max(smax, axis=1, keepdims=True))
                alpha = jnp.exp2(m_old - m_new)
                ps = [jnp.exp2(x - m_new) for x in cols]
                for c, x in enumerate(ps):
                    p_ref[h, rows, c * LANES:(c + 1) * LANES] = x.astype(BF16)
                m_ref[rows, hs] = m_new
                acc_ref[rows, hs] = acc_ref[rows, hs] * alpha
                if not ones_col:
                    l_ref[rows, hs] = l_ref[rows, hs] * alpha + functools.reduce(jnp.add, ps)
            acc_ref[:, hs] += jnp.dot(p_ref[h], v_ref[pl.ds(off, tk), hs], preferred_element_type=F32)
        return carry

    lax.fori_loop(0, n_full, functools.partial(step, masked=False), 0)
    lax.fori_loop(n_full, n_all, functools.partial(step, masked=True), 0)
    for h in range(HEADS):
        hs = slice(h * dv, (h + 1) * dv)
        acc = acc_ref[:, hs]
        if ones_col:
            l = acc[:, BIAS_LANE:BIAS_LANE + 1]
        else:
            l = jnp.sum(l_ref[:, hs], axis=1, keepdims=True)
        o_ref[:, hs] = (acc / l * g_ref[:, hs]).astype(BF16)


def _attn(q, k, v, g, *, name, batch, tq_total, tk_total, tq, tk, dk, past, kv_len, chunk, ones_col):
    nq = tq_total // tq
    dv = LANES
    kern = functools.partial(_attn_kernel, tk=tk, dk=dk, past=past, kv_len=kv_len, chunk=chunk,
                             ones_col=ones_col)
    stats = pltpu.VMEM((tq, HEADS * dv), F32)
    return pl.pallas_call(
        kern,
        grid=(batch, nq),
        in_specs=[pl.BlockSpec((tq, HEADS * dk), lambda b, i: (b * nq + i, 0)),
                  pl.BlockSpec((tk_total, HEADS * dk), lambda b, i: (b, 0)),
                  pl.BlockSpec((tk_total, HEADS * dv), lambda b, i: (b, 0)),
                  pl.BlockSpec((tq, HEADS * dv), lambda b, i: (b * nq + i, 0))],
        out_specs=pl.BlockSpec((tq, HEADS * dv), lambda b, i: (b * nq + i, 0)),
        out_shape=jax.ShapeDtypeStruct((batch * tq_total, HEADS * dv), BF16),
        scratch_shapes=[pltpu.VMEM((HEADS, tq, tk), F32), pltpu.VMEM((HEADS, tq, tk), BF16),
                        stats, stats, stats],
        compiler_params=_params(("arbitrary", "arbitrary")),
        name=name,
    )(q, k, v, g)


def _out_kernel(mla_ref, fox_ref, mem_ref, x_ref, wmla_ref, wfox_ref, wmem_ref, g_ref, b_ref, y_ref):
    out = jnp.dot(mla_ref[...], wmla_ref[...], preferred_element_type=F32)
    out += jnp.dot(fox_ref[...], wfox_ref[...], preferred_element_type=F32)
    out += jnp.dot(mem_ref[...], wmem_ref[...], preferred_element_type=F32)
    z = DEEPNORM_ALPHA * x_ref[...] + out
    mu = jnp.mean(z, axis=-1, keepdims=True)
    zc = z - mu
    var = jnp.mean(zc * zc, axis=-1, keepdims=True)
    y_ref[...] = zc * lax.rsqrt(var + NORM_EPS) * g_ref[...] + b_ref[...]


def _out(mla, fox, mem, x, w, *, tm):
    rows = x.shape[0]
    rows_spec = lambda width: pl.BlockSpec((tm, width), lambda r: (r, 0))
    full_spec = lambda a: pl.BlockSpec(a.shape, lambda r: (0, 0))
    return pl.pallas_call(
        _out_kernel,
        grid=(rows // tm,),
        in_specs=[rows_spec(mla.shape[1]), rows_spec(fox.shape[1]), rows_spec(mem.shape[1]),
                  rows_spec(D_MODEL), full_spec(w['wo_mla']), full_spec(w['wo_fox']),
                  full_spec(w['wo_mem']), full_spec(w['ln_g']), full_spec(w['ln_b'])],
        out_specs=rows_spec(D_MODEL),
        out_shape=jax.ShapeDtypeStruct((rows, D_MODEL), F32),
        compiler_params=_params(("arbitrary",)),
        name="out_ln",
    )(mla, fox, mem, x, w['wo_mla'], w['wo_fox'], w['wo_mem'], w['ln_g'], w['ln_b'])


def _memkv_kernel(m_ref, w_ref, k_ref, v_ref):
    kv = jnp.dot(m_ref[...].astype(BF16), w_ref[...], preferred_element_type=F32)
    k_ref[...] = kv[:, :HEADS * MEM_DIM]
    v_ref[...] = kv[:, HEADS * MEM_DIM:]


def _mem_kv(mem, wmem):
    rows = mem.shape[0]
    width = HEADS * MEM_DIM
    return pl.pallas_call(
        _memkv_kernel,
        grid=(rows // N_MEM,),
        in_specs=[pl.BlockSpec((N_MEM, D_MODEL), lambda r: (r, 0)),
                  pl.BlockSpec(wmem.shape, lambda r: (0, 0))],
        out_specs=[pl.BlockSpec((N_MEM, width), lambda r: (r, 0))] * 2,
        out_shape=[jax.ShapeDtypeStruct((rows, width), F32)] * 2,
        compiler_params=_params(("arbitrary",)),
        name="mem_kv",
    )(mem, wmem)


def _rope_tables(pos):
    half = MLA_ROPE // 2
    inv = ROPE_THETA ** (-jnp.arange(half, dtype=F32) / half)
    ang = pos.astype(F32)[:, None] * inv[None, :]
    cos, sin = jnp.cos(ang), jnp.sin(ang)
    n = pos.shape[0]
    return (jnp.concatenate([cos, cos, jnp.ones((n, LANES - MLA_ROPE), F32)], axis=1),
            jnp.concatenate([-sin, sin, jnp.zeros((n, LANES - MLA_ROPE), F32)], axis=1))


def _layer_weights(w_in, b_f, qg, kvg, w_uq, w_ukv, w_out, ln_g, ln_b):
    c = [0]
    for n in (MLA_Q_RANK, MLA_KV_RANK, MLA_ROPE, HEADS * MLA_V, 256, 256, 256, HEADS, 256, 256, 256):
        c.append(c[-1] + n)
    s = lambda i: w_in[:, c[i]:c[i + 1]]
    win = jnp.concatenate([s(0), s(1), s(3), s(4), s(5), s(6), s(8), s(9), s(10), s(2), s(7),
                           jnp.zeros((D_MODEL, LANES - MLA_ROPE - HEADS), F32)], axis=1).astype(BF16)
    wuq = jnp.pad(w_uq.reshape(MLA_Q_RANK, HEADS, MLA_NOPE + MLA_ROPE),
                  ((0, 0), (0, 0), (0, MLA_DK - MLA_NOPE - MLA_ROPE))).reshape(MLA_Q_RANK, HEADS * MLA_DK)
    wo_fox = jnp.pad(w_out[HEADS * MLA_V:HEADS * MLA_V + HEADS * FOX_DIM].reshape(HEADS, FOX_DIM, D_MODEL),
                     ((0, 0), (0, FOX_DK - FOX_DIM), (0, 0))).reshape(HEADS * FOX_DK, D_MODEL)
    bf = jnp.zeros((1, LANES), F32).at[0, BIAS_LANE:BIAS_LANE + HEADS].set(b_f)
    return dict(win=win, wuq=wuq.astype(BF16), wukv=w_ukv.astype(BF16),
                qg=qg.reshape(1, -1), kvg=kvg.reshape(1, -1), bf=bf,
                wo_mla=w_out[:HEADS * MLA_V].astype(BF16), wo_fox=wo_fox.astype(BF16),
                wo_mem=w_out[HEADS * MLA_V + HEADS * FOX_DIM:].astype(BF16),
                ln_g=ln_g.reshape(1, -1), ln_b=ln_b.reshape(1, -1))


def _pad_rows(a, batch, seq, seq_pad):
    a = a.reshape(batch, seq, a.shape[-1])
    return jnp.pad(a, ((0, 0), (0, seq_pad - seq), (0, 0))).reshape(batch * seq_pad, a.shape[-1])


def kernel(x_prompt, x_sample, cache_mla_latent, cache_mla_krope, cache_fox_k, cache_fox_v, cache_fox_logf,
           cache_mem_k, cache_mem_v, mem_prompt, w_in, b_fox_f, mla_q_norm, mla_kv_norm, w_uq, w_ukv,
           w_mem_kv, w_out, ln_g, ln_b):
    batch, seq, _ = x_prompt.shape
    dec_batch, dec_seq, _ = x_sample.shape
    depth = w_in.shape[0]
    past = cache_mla_latent.shape[2]
    kv_len = past + dec_seq
    tk_s = 768
    kv_pad = -(-kv_len // tk_s) * tk_s

    cos_p, sin_p = _rope_tables(jnp.arange(seq))
    cos_s, sin_s = _rope_tables(past + jnp.arange(dec_seq))
    zero_carry = jnp.zeros((batch, 8, LANES), F32)

    yp = x_prompt.reshape(batch * seq, D_MODEL)
    ys = x_sample.reshape(dec_batch * dec_seq, D_MODEL)
    mem2d = mem_prompt.reshape(batch * N_MEM, D_MODEL)
    p_out, s_out, p_mk, p_mv = [], [], [], []
    for l in range(depth):
        w = _layer_weights(w_in[l], b_fox_f[l], mla_q_norm[l], mla_kv_norm[l], w_uq[l], w_ukv[l],
                           w_out[l], ln_g[l], ln_b[l])
        mk, mv = _mem_kv(mem2d, w_mem_kv[l].astype(BF16))
        (lat, kr, fk, fv, logf, q, k, v, fqa, fka, fva, gmla, gfox, mmem) = _proj(
            yp, zero_carry, mk.reshape(batch, N_MEM, -1), mv.reshape(batch, N_MEM, -1), cos_p, sin_p, w,
            batch=batch, seq=seq, tm=256)
        o_mla = _attn(q, k, v, gmla, name="attn_mla", batch=batch, tq_total=seq, tk_total=seq, tq=512, tk=512,
                      dk=MLA_DK, past=0, kv_len=seq, chunk=CHUNK, ones_col=False)
        o_fox = _attn(fqa, fka, fva, gfox, name="attn_fox", batch=batch, tq_total=seq, tk_total=seq, tq=512,
                      tk=512, dk=FOX_DK, past=0, kv_len=seq, chunk=1, ones_col=True)
        yp = _out(o_mla, o_fox, mmem, yp, w, tm=512)
        p_out.append((lat, kr, fk, fv, logf))
        p_mk.append(mk)
        p_mv.append(mv)

        rows_c = dec_batch * past
        logf_c = jnp.pad(cache_fox_logf[l].reshape(rows_c, HEADS),
                         ((0, 0), (BIAS_LANE, LANES - BIAS_LANE - HEADS)))
        ck, cv, cfka, cfva, carry = _cache_prep(
            cache_mla_latent[l].reshape(rows_c, -1), cache_mla_krope[l].reshape(rows_c, -1),
            cache_fox_k[l].reshape(rows_c, -1), cache_fox_v[l].reshape(rows_c, -1), logf_c, w['wukv'],
            batch=dec_batch, seq=past, tm=512)
        (lat, kr, fk, fv, logf, q, k, v, fqa, fka, fva, gmla, gfox, mmem) = _proj(
            ys, carry, cache_mem_k[l].reshape(dec_batch, N_MEM, -1), cache_mem_v[l].reshape(dec_batch, N_MEM, -1),
            cos_s, sin_s, w, batch=dec_batch, seq=dec_seq, tm=dec_seq)

        def joined(cached, new):
            width = cached.shape[-1]
            a = jnp.concatenate([cached.reshape(dec_batch, past, width),
                                 new.reshape(dec_batch, dec_seq, width)], axis=1)
            return _pad_rows(a, dec_batch, kv_len, kv_pad)

        o_mla = _attn(q, joined(ck, k), joined(cv, v), gmla, name="attn_mla_s", batch=dec_batch,
                      tq_total=dec_seq, tk_total=kv_pad, tq=dec_seq, tk=tk_s, dk=MLA_DK, past=past,
                      kv_len=kv_len, chunk=CHUNK, ones_col=False)
        o_fox = _attn(fqa, joined(cfka, fka), joined(cfva, fva), gfox, name="attn_fox_s", batch=dec_batch,
                      tq_total=dec_seq, tk_total=kv_pad, tq=dec_seq, tk=tk_s, dk=FOX_DK, past=past,
                      kv_len=kv_len, chunk=1, ones_col=True)
        ys = _out(o_mla, o_fox, mmem, ys, w, tm=dec_batch * dec_seq)
        s_out.append((lat, kr, fk, fv, logf))

    def stack(rows, i, shape):
        return jnp.stack([r[i].reshape(shape) for r in rows], axis=0)

    pb, sb = (batch, seq), (dec_batch, dec_seq)
    return (yp.reshape(batch, seq, D_MODEL), ys.reshape(dec_batch, dec_seq, D_MODEL),
            stack(p_out, 0, pb + (MLA_KV_RANK,)), stack(p_out, 1, pb + (MLA_ROPE,)),
            stack(p_out, 2, pb + (HEADS, FOX_DIM)), stack(p_out, 3, pb + (HEADS, FOX_DIM)),
            stack(p_out, 4, pb + (HEADS,)),
            jnp.stack([a.reshape(batch, N_MEM, HEADS, MEM_DIM) for a in p_mk], axis=0),
            jnp.stack([a.reshape(batch, N_MEM, HEADS, MEM_DIM) for a in p_mv], axis=0),
            stack(s_out, 0, sb + (MLA_KV_RANK,)), stack(s_out, 1, sb + (MLA_ROPE,)),
            stack(s_out, 2, sb + (HEADS, FOX_DIM)), stack(s_out, 3, sb + (HEADS, FOX_DIM)),
            stack(s_out, 4, sb + (HEADS,)))
```

```python
import functools

import jax
import jax.numpy as jnp
from jax import lax
from jax.experimental import pallas as pl
from jax.experimental.pallas import tpu as pltpu

F32 = jnp.float32
BF16 = jnp.bfloat16

D_MODEL = 1024
CHUNK = 64
N_MEM = 256
HEADS = 4
MLA_NOPE = 128
MLA_ROPE = 64
MLA_V = 128
MLA_Q_RANK = 384
MLA_KV_RANK = 256
FOX_DIM = 64
MEM_DIM = 64
ROPE_THETA = 10000.0
NORM_EPS = 1e-6
NEG_INF = -1e30
MLA_SCALE = (MLA_NOPE + MLA_ROPE) ** -0.5
FOX_SCALE = FOX_DIM ** -0.5
MEM_SCALE = MEM_DIM ** -0.5
DEPTH = 2
DEEPNORM_ALPHA = (2 * DEPTH) ** 0.25
LOG2E = 1.4426950408889634

LANES = 128
MLA_DK = 256
FOX_DK = 128
BIAS_LANE = 64
VMEM_LIMIT = 52 * 1024 * 1024
ATTN_ROW_BLOCK = 32

O_CQ, O_CKV, O_GMLA, O_FQ, O_FK, O_FV, O_GFOX, O_MQ, O_GMEM, O_KR, O_END = (
    0, 384, 640, 1152, 1408, 1664, 1920, 2176, 2432, 2688, 2816)


def _params(sem):
    return pltpu.CompilerParams(dimension_semantics=sem, vmem_limit_bytes=VMEM_LIMIT)


def _silu(x):
    return x * jax.nn.sigmoid(x)


def _rms(x, g):
    return x * lax.rsqrt(jnp.mean(x * x, axis=-1, keepdims=True) + NORM_EPS) * g


def _rope128(x, cos, sin):
    lane = lax.broadcasted_iota(jnp.int32, x.shape, 1)
    swapped = jnp.where(lane < MLA_ROPE // 2, pltpu.roll(x, LANES - MLA_ROPE // 2, 1),
                        pltpu.roll(x, MLA_ROPE // 2, 1))
    return x * cos + swapped * sin


def _cumsum_rows(x):
    n = x.shape[0]
    row = lax.broadcasted_iota(jnp.int32, x.shape, 0)
    shift = 1
    while shift < n:
        x = x + jnp.where(row >= shift, pltpu.roll(x, shift, 0), 0.0)
        shift *= 2
    return x


def _split3(x):
    hi = x.astype(BF16).astype(F32)
    r = x - hi
    mid = r.astype(BF16).astype(F32)
    return hi, mid, r - mid


def _head_cols(x, h):
    col = x[:, (h // 2) * LANES:(h // 2 + 1) * LANES]
    return pltpu.roll(col, FOX_DIM, 1) if h % 2 else col


def _fox_kv_aug(fk, fv, cum, fka_ref, fva_ref):
    rows = fk.shape[0]
    lane = lax.broadcasted_iota(jnp.int32, (rows, LANES), 1)
    for h in range(HEADS):
        c = jnp.broadcast_to(cum[:, BIAS_LANE + h:BIAS_LANE + h + 1], (rows, LANES)) * LOG2E
        hi, mid, lo = _split3(c)
        kb = jnp.where(lane < BIAS_LANE + 3, 1.0,
             jnp.where(lane == BIAS_LANE + 3, -hi,
             jnp.where(lane == BIAS_LANE + 4, -mid,
             jnp.where(lane == BIAS_LANE + 5, -lo, 0.0))))
        ka = jnp.where(lane < BIAS_LANE, _head_cols(fk, h), kb)
        va = jnp.where(lane < BIAS_LANE, _head_cols(fv, h),
                       jnp.where(lane == BIAS_LANE, 1.0, 0.0))
        fka_ref[:, h * LANES:(h + 1) * LANES] = ka.astype(BF16)
        fva_ref[:, h * LANES:(h + 1) * LANES] = va.astype(BF16)


def _mla_kv(lat, kr128, wukv_ref, k_ref, v_ref):
    rows = lat.shape[0]
    kv = jnp.dot(lat.astype(BF16), wukv_ref[...], preferred_element_type=F32)
    lane = lax.broadcasted_iota(jnp.int32, (rows, LANES), 1)
    krb = jnp.where(lane < MLA_ROPE, kr128, 0.0).astype(BF16)
    for h in range(HEADS):
        k_ref[:, h * MLA_DK:h * MLA_DK + MLA_NOPE] = kv[:, 2 * h * LANES:(2 * h + 1) * LANES].astype(BF16)
        k_ref[:, h * MLA_DK + MLA_NOPE:(h + 1) * MLA_DK] = krb
        v_ref[:, h * MLA_V:(h + 1) * MLA_V] = kv[:, (2 * h + 1) * LANES:(2 * h + 2) * LANES].astype(BF16)


def _proj_kernel(x_ref, win_ref, wuq_ref, wukv_ref, qg_ref, kvg_ref, bf_ref, cos_ref, sin_ref,
                 cin_ref, mk_ref, mv_ref,
                 lat_ref, kr_ref, fk_ref, fv_ref, logf_ref, q_ref, k_ref, v_ref,
                 fqa_ref, fka_ref, fva_ref, gmla_ref, gfox_ref, mmem_ref, carry_ref):
    tm = x_ref.shape[0]

    @pl.when(pl.program_id(1) == 0)
    def _():
        carry_ref[...] = cin_ref[0]

    xb = x_ref[...].astype(BF16)

    def seg(lo, hi):
        return jnp.dot(xb, win_ref[:, lo:hi], preferred_element_type=F32)

    cos = cos_ref[...]
    sin = sin_ref[...]
    lane = lax.broadcasted_iota(jnp.int32, (tm, LANES), 1)

    cqn = _rms(seg(O_CQ, O_CKV), qg_ref[...])
    q = jnp.dot(cqn.astype(BF16), wuq_ref[...], preferred_element_type=F32) * (MLA_SCALE * LOG2E)
    for h in range(HEADS):
        q_ref[:, h * MLA_DK:h * MLA_DK + MLA_NOPE] = q[:, 2 * h * LANES:(2 * h + 1) * LANES].astype(BF16)
        q_ref[:, h * MLA_DK + MLA_NOPE:(h + 1) * MLA_DK] = _rope128(
            q[:, (2 * h + 1) * LANES:(2 * h + 2) * LANES], cos, sin).astype(BF16)

    lat = _rms(seg(O_CKV, O_GMLA), kvg_ref[...])
    lat_ref[...] = lat
    krl = seg(O_KR, O_END)
    kr128 = _rope128(krl, cos, sin)
    kr_ref[...] = kr128[:, :MLA_ROPE]
    _mla_kv(lat, kr128, wukv_ref, k_ref, v_ref)

    z = krl + bf_ref[...]
    logf = jnp.minimum(z, 0.0) - jnp.log1p(jnp.exp(-jnp.abs(z)))
    logf = jnp.where((lane >= BIAS_LANE) & (lane < BIAS_LANE + HEADS), logf, 0.0)
    logf_ref[...] = pltpu.roll(logf, LANES - BIAS_LANE, 1)[:, :HEADS]
    cum = _cumsum_rows(logf) + carry_ref[0:1, :]
    carry_ref[...] = jnp.broadcast_to(cum[tm - 1:tm, :], carry_ref.shape)

    fq = seg(O_FQ, O_FK)
    fk = seg(O_FK, O_FV)
    fv = seg(O_FV, O_GFOX)
    fk_ref[...] = fk
    fv_ref[...] = fv
    _fox_kv_aug(fk, fv, cum, fka_ref, fva_ref)
    gfox = _silu(seg(O_GFOX, O_MQ))
    for h in range(HEADS):
        c = jnp.broadcast_to(cum[:, BIAS_LANE + h:BIAS_LANE + h + 1], (tm, LANES)) * LOG2E
        hi, mid, lo = _split3(c)
        qb = jnp.where(lane == BIAS_LANE, hi,
             jnp.where(lane == BIAS_LANE + 1, mid,
             jnp.where(lane == BIAS_LANE + 2, lo,
             jnp.where(lane < BIAS_LANE + 6, 1.0, 0.0))))
        qa = jnp.where(lane < BIAS_LANE, _head_cols(fq, h) * (FOX_SCALE * LOG2E), qb)
        fqa_ref[:, h * LANES:(h + 1) * LANES] = qa.astype(BF16)
        gfox_ref[:, h * LANES:(h + 1) * LANES] = jnp.where(lane < BIAS_LANE, _head_cols(gfox, h), 0.0)

    gmla_ref[...] = _silu(seg(O_GMLA, O_FQ))

    mq = seg(O_MQ, O_GMEM).astype(BF16)
    gmem = _silu(seg(O_GMEM, O_KR))
    mk = mk_ref[0]
    mv = mv_ref[0].astype(BF16)
    key_head = lax.broadcasted_iota(jnp.int32, mk.shape, 1) // MEM_DIM
    out_head = lax.broadcasted_iota(jnp.int32, (tm, HEADS * MEM_DIM), 1) // MEM_DIM
    o_mem = jnp.zeros((tm, HEADS * MEM_DIM), F32)
    for h in range(HEADS):
        mk_h = jnp.where(key_head == h, mk, 0.0).astype(BF16)
        s = lax.dot_general(mq, mk_h, (((1,), (1,)), ((), ())), preferred_element_type=F32) * MEM_SCALE
        e = jnp.exp(s - jnp.max(s, axis=-1, keepdims=True))
        p = e / jnp.sum(e, axis=-1, keepdims=True)
        o_h = jnp.dot(p.astype(BF16), mv, preferred_element_type=F32)
        o_mem = jnp.where(out_head == h, o_h, o_mem)
    mmem_ref[...] = (o_mem * gmem).astype(BF16)


KV_OUTPUTS = (6, 7, 9, 10)


def _proj(x, cin, mk, mv, cos, sin, w, *, batch, seq, tm, kv_bufs=None, kv_seq=None, kv_start=0):
    rows = batch * seq
    nt = seq // tm
    row_map = lambda b, i: (b * nt + i, 0)
    full = lambda b, i: (0, 0)
    per_b = lambda b, i: (b, 0, 0)
    tab_map = lambda b, i: (i, 0)

    def rows_spec(width):
        return pl.BlockSpec((tm, width), row_map)

    def full_spec(a):
        return pl.BlockSpec(a.shape, full, pipeline_mode=pl.Buffered(1))

    out_widths = [(MLA_KV_RANK, F32), (MLA_ROPE, F32), (HEADS * FOX_DIM, F32), (HEADS * FOX_DIM, F32),
                  (HEADS, F32), (HEADS * MLA_DK, BF16), (HEADS * MLA_DK, BF16), (HEADS * MLA_V, BF16),
                  (HEADS * FOX_DK, BF16), (HEADS * FOX_DK, BF16), (HEADS * FOX_DK, BF16),
                  (HEADS * MLA_V, F32), (HEADS * FOX_DK, F32), (HEADS * MEM_DIM, BF16)]
    out_specs = [rows_spec(wd) for wd, _ in out_widths]
    out_shape = [jax.ShapeDtypeStruct((rows, wd), dt) for wd, dt in out_widths]
    n_in = 12
    extra_in, extra_specs, aliases, kern = [], [], {}, _proj_kernel
    if kv_bufs is not None:
        kv_map = lambda b, i: (b * (kv_seq // tm) + kv_start // tm + i, 0)
        for n, (pos, buf) in enumerate(zip(KV_OUTPUTS, kv_bufs)):
            out_specs[pos] = pl.BlockSpec((tm, out_widths[pos][0]), kv_map)
            out_shape[pos] = jax.ShapeDtypeStruct(buf.shape, buf.dtype)
            aliases[n_in + n] = pos
        extra_in = list(kv_bufs)
        extra_specs = [pl.BlockSpec(memory_space=pl.ANY)] * len(kv_bufs)

        def kern(*refs):
            _proj_kernel(*refs[:n_in], *refs[n_in + len(kv_bufs):])

    return pl.pallas_call(
        kern,
        grid=(batch, nt),
        in_specs=[rows_spec(D_MODEL), full_spec(w['win']), full_spec(w['wuq']), full_spec(w['wukv']),
                  full_spec(w['qg']), full_spec(w['kvg']), full_spec(w['bf']),
                  pl.BlockSpec((tm, LANES), tab_map), pl.BlockSpec((tm, LANES), tab_map),
                  pl.BlockSpec((1, 8, LANES), per_b),
                  pl.BlockSpec((1, N_MEM, HEADS * MEM_DIM), per_b),
                  pl.BlockSpec((1, N_MEM, HEADS * MEM_DIM), per_b)] + extra_specs,
        out_specs=out_specs,
        out_shape=out_shape,
        input_output_aliases=aliases,
        scratch_shapes=[pltpu.VMEM((8, LANES), F32)],
        compiler_params=_params(("arbitrary", "arbitrary")),
        name="proj",
    )(x, w['win'], w['wuq'], w['wukv'], w['qg'], w['kvg'], w['bf'], cos, sin, cin, mk, mv, *extra_in)


def _cache_kernel(lat_ref, kr_ref, fk_ref, fv_ref, logf_ref, wukv_ref,
                  k_ref, v_ref, fka_ref, fva_ref, cout_ref, carry_ref, *, nt):
    tm = lat_ref.shape[0]
    i = pl.program_id(1)

    @pl.when(i == 0)
    def _():
        carry_ref[...] = jnp.zeros_like(carry_ref)

    @pl.when(i < nt)
    def _():
        kr128 = jnp.concatenate([kr_ref[...], jnp.zeros((tm, LANES - MLA_ROPE), F32)], axis=1)
        _mla_kv(lat_ref[...], kr128, wukv_ref, k_ref, v_ref)
        cum = _cumsum_rows(logf_ref[...]) + carry_ref[0:1, :]
        carry_ref[...] = jnp.broadcast_to(cum[tm - 1:tm, :], carry_ref.shape)
        cout_ref[0] = carry_ref[...]
        _fox_kv_aug(fk_ref[...], fv_ref[...], cum, fka_ref, fva_ref)

    @pl.when(i >= nt)
    def _():
        for ref in (k_ref, v_ref, fka_ref, fva_ref):
            ref[...] = jnp.zeros_like(ref)


def _cache_prep(lat, kr, fk, fv, logf128, wukv, *, batch, seq, seq_pad, tm):
    nt = seq // tm
    nt_pad = seq_pad // tm
    in_map = lambda b, i: (b * nt + jnp.minimum(i, nt - 1), 0)
    in_spec = lambda width: pl.BlockSpec((tm, width), in_map)
    out_spec = lambda width: pl.BlockSpec((tm, width), lambda b, i: (b * nt_pad + i, 0))
    out_widths = [HEADS * MLA_DK, HEADS * MLA_V, HEADS * FOX_DK, HEADS * FOX_DK]
    return pl.pallas_call(
        functools.partial(_cache_kernel, nt=nt),
        grid=(batch, nt_pad),
        in_specs=[in_spec(MLA_KV_RANK), in_spec(MLA_ROPE), in_spec(HEADS * FOX_DIM),
                  in_spec(HEADS * FOX_DIM), in_spec(LANES),
                  pl.BlockSpec(wukv.shape, lambda b, i: (0, 0))],
        out_specs=[out_spec(wd) for wd in out_widths]
                  + [pl.BlockSpec((1, 8, LANES), lambda b, i: (b, 0, 0))],
        out_shape=[jax.ShapeDtypeStruct((batch * seq_pad, wd), BF16) for wd in out_widths]
                  + [jax.ShapeDtypeStruct((batch, 8, LANES), F32)],
        scratch_shapes=[pltpu.VMEM((8, LANES), F32)],
        compiler_params=_params(("arbitrary", "arbitrary")),
        name="cache_prep",
    )(lat, kr, fk, fv, logf128, wukv)


def _attn_kernel(q_ref, k_ref, v_ref, g_ref, o_ref, s_ref, p_ref, m_ref, l_ref, acc_ref,
                 *, tk, dk, past, kv_len, chunk, ones_col):
    tq = q_ref.shape[0]
    dv = LANES
    rb = min(ATTN_ROW_BLOCK, tq)
    q0 = past + pl.program_id(1) * tq
    n_full = jnp.minimum((q0 // chunk + 1) * chunk, kv_len) // tk
    n_all = (jnp.minimum(((q0 + tq - 1) // chunk + 1) * chunk, kv_len) + tk - 1) // tk
    lane = lax.broadcasted_iota(jnp.int32, (rb, LANES), 1)

    m_ref[...] = jnp.full(m_ref.shape, NEG_INF, F32)
    l_ref[...] = jnp.zeros(l_ref.shape, F32)
    acc_ref[...] = jnp.zeros(acc_ref.shape, F32)

    def step(j, carry, masked):
        off = pl.multiple_of(j * tk, tk)
        for h in range(HEADS):
            s_ref[h] = lax.dot_general(q_ref[:, h * dk:(h + 1) * dk], k_ref[pl.ds(off, tk), h * dk:(h + 1) * dk],
                                       (((1,), (1,)), ((), ())), preferred_element_type=F32)
        for h in range(HEADS):
            hs = slice(h * dv, (h + 1) * dv)
            for r in range(0, tq, rb):
                rows = slice(r, r + rb)
                cols = [s_ref[h, rows, c * LANES:(c + 1) * LANES] for c in range(tk // LANES)]
                if masked:
                    qpos = q0 + r + lax.broadcasted_iota(jnp.int32, (rb, LANES), 0)
                    limit = jnp.minimum((qpos // chunk + 1) * chunk, kv_len) - off
                    cols = [jnp.where(lane + c * LANES < limit, x, NEG_INF) for c, x in enumerate(cols)]
                smax = functools.reduce(jnp.maximum, cols)
                m_old = m_ref[rows, hs]
                m_new = jnp.maximum(m_old, jnp.max(smax, axis=1, keepdims=True))
                alpha = jnp.exp2(m_old - m_new)
                ps = [jnp.exp2(x - m_new) for x in cols]
                for c, x in enumerate(ps):
                    p_ref[h, rows, c * LANES:(c + 1) * LANES] = x.astype(BF16)
                m_ref[rows, hs] = m_new
                acc_ref[rows, hs] = acc_ref[rows, hs] * alpha
                if not ones_col:
                    l_ref[rows, hs] = l_ref[rows, hs] * alpha + functools.reduce(jnp.add, ps)
            acc_ref[:, hs] += jnp.dot(p_ref[h], v_ref[pl.ds(off, tk), hs], preferred_element_type=F32)
        return carry

    lax.fori_loop(0, n_full, functools.partial(step, masked=False), 0)
    lax.fori_loop(n_full, n_all, functools.partial(step, masked=True), 0)
    for h in range(HEADS):
        hs = slice(h * dv, (h + 1) * dv)
        acc = acc_ref[:, hs]
        if ones_col:
            l = acc[:, BIAS_LANE:BIAS_LANE + 1]
        else:
            l = jnp.sum(l_ref[:, hs], axis=1, keepdims=True)
        o_ref[:, hs] = (acc / l * g_ref[:, hs]).astype(BF16)


def _attn(q, k, v, g, *, name, batch, tq_total, tk_total, tq, tk, dk, past, kv_len, chunk, ones_col):
    nq = tq_total // tq
    dv = LANES
    kern = functools.partial(_attn_kernel, tk=tk, dk=dk, past=past, kv_len=kv_len, chunk=chunk,
                             ones_col=ones_col)
    stats = pltpu.VMEM((tq, HEADS * dv), F32)
    return pl.pallas_call(
        kern,
        grid=(batch, nq),
        in_specs=[pl.BlockSpec((tq, HEADS * dk), lambda b, i: (b * nq + i, 0)),
                  pl.BlockSpec((tk_total, HEADS * dk), lambda b, i: (b, 0)),
                  pl.BlockSpec((tk_total, HEADS * dv), lambda b, i: (b, 0)),
                  pl.BlockSpec((tq, HEADS * dv), lambda b, i: (b * nq + i, 0))],
        out_specs=pl.BlockSpec((tq, HEADS * dv), lambda b, i: (b * nq + i, 0)),
        out_shape=jax.ShapeDtypeStruct((batch * tq_total, HEADS * dv), BF16),
        scratch_shapes=[pltpu.VMEM((HEADS, tq, tk), F32), pltpu.VMEM((HEADS, tq, tk), BF16),
                        stats, stats, stats],
        compiler_params=_params(("arbitrary", "arbitrary")),
        name=name,
    )(q, k, v, g)


def _out_kernel(mla_ref, fox_ref, mem_ref, x_ref, wmla_ref, wfox_ref, wmem_ref, g_ref, b_ref, y_ref):
    out = jnp.dot(mla_ref[...], wmla_ref[...], preferred_element_type=F32)
    out += jnp.dot(fox_ref[...], wfox_ref[...], preferred_element_type=F32)
    out += jnp.dot(mem_ref[...], wmem_ref[...], preferred_element_type=F32)
    z = DEEPNORM_ALPHA * x_ref[...] + out
    mu = jnp.mean(z, axis=-1, keepdims=True)
    zc = z - mu
    var = jnp.mean(zc * zc, axis=-1, keepdims=True)
    y_ref[...] = zc * lax.rsqrt(var + NORM_EPS) * g_ref[...] + b_ref[...]


def _out(mla, fox, mem, x, w, *, tm):
    rows = x.shape[0]
    rows_spec = lambda width: pl.BlockSpec((tm, width), lambda r: (r, 0))
    full_spec = lambda a: pl.BlockSpec(a.shape, lambda r: (0, 0))
    return pl.pallas_call(
        _out_kernel,
        grid=(rows // tm,),
        in_specs=[rows_spec(mla.shape[1]), rows_spec(fox.shape[1]), rows_spec(mem.shape[1]),
                  rows_spec(D_MODEL), full_spec(w['wo_mla']), full_spec(w['wo_fox']),
                  full_spec(w['wo_mem']), full_spec(w['ln_g']), full_spec(w['ln_b'])],
        out_specs=rows_spec(D_MODEL),
        out_shape=jax.ShapeDtypeStruct((rows, D_MODEL), F32),
        compiler_params=_params(("arbitrary",)),
        name="out_ln",
    )(mla, fox, mem, x, w['wo_mla'], w['wo_fox'], w['wo_mem'], w['ln_g'], w['ln_b'])


def _memkv_kernel(m_ref, w_ref, k_ref, v_ref):
    kv = jnp.dot(m_ref[...].astype(BF16), w_ref[...], preferred_element_type=F32)
    k_ref[...] = kv[:, :HEADS * MEM_DIM]
    v_ref[...] = kv[:, HEADS * MEM_DIM:]


def _mem_kv(mem, wmem):
    rows = mem.shape[0]
    width = HEADS * MEM_DIM
    return pl.pallas_call(
        _memkv_kernel,
        grid=(rows // N_MEM,),
        in_specs=[pl.BlockSpec((N_MEM, D_MODEL), lambda r: (r, 0)),
                  pl.BlockSpec(wmem.shape, lambda r: (0, 0))],
        out_specs=[pl.BlockSpec((N_MEM, width), lambda r: (r, 0))] * 2,
        out_shape=[jax.ShapeDtypeStruct((rows, width), F32)] * 2,
        compiler_params=_params(("arbitrary",)),
        name="mem_kv",
    )(mem, wmem)


def _rope_tables(pos):
    half = MLA_ROPE // 2
    inv = ROPE_THETA ** (-jnp.arange(half, dtype=F32) / half)
    ang = pos.astype(F32)[:, None] * inv[None, :]
    cos, sin = jnp.cos(ang), jnp.sin(ang)
    n = pos.shape[0]
    return (jnp.concatenate([cos, cos, jnp.ones((n, LANES - MLA_ROPE), F32)], axis=1),
            jnp.concatenate([-sin, sin, jnp.zeros((n, LANES - MLA_ROPE), F32)], axis=1))


def _layer_weights(w_in, b_f, qg, kvg, w_uq, w_ukv, w_out, ln_g, ln_b):
    c = [0]
    for n in (MLA_Q_RANK, MLA_KV_RANK, MLA_ROPE, HEADS * MLA_V, 256, 256, 256, HEADS, 256, 256, 256):
        c.append(c[-1] + n)
    s = lambda i: w_in[:, c[i]:c[i + 1]]
    win = jnp.concatenate([s(0), s(1), s(3), s(4), s(5), s(6), s(8), s(9), s(10), s(2), s(7),
                           jnp.zeros((D_MODEL, LANES - MLA_ROPE - HEADS), F32)], axis=1).astype(BF16)
    wuq = jnp.pad(w_uq.reshape(MLA_Q_RANK, HEADS, MLA_NOPE + MLA_ROPE),
                  ((0, 0), (0, 0), (0, MLA_DK - MLA_NOPE - MLA_ROPE))).reshape(MLA_Q_RANK, HEADS * MLA_DK)
    wo_fox = jnp.pad(w_out[HEADS * MLA_V:HEADS * MLA_V + HEADS * FOX_DIM].reshape(HEADS, FOX_DIM, D_MODEL),
                     ((0, 0), (0, FOX_DK - FOX_DIM), (0, 0))).reshape(HEADS * FOX_DK, D_MODEL)
    bf = jnp.zeros((1, LANES), F32).at[0, BIAS_LANE:BIAS_LANE + HEADS].set(b_f)
    return dict(win=win, wuq=wuq.astype(BF16), wukv=w_ukv.astype(BF16),
                qg=qg.reshape(1, -1), kvg=kvg.reshape(1, -1), bf=bf,
                wo_mla=w_out[:HEADS * MLA_V].astype(BF16), wo_fox=wo_fox.astype(BF16),
                wo_mem=w_out[HEADS * MLA_V + HEADS * FOX_DIM:].astype(BF16),
                ln_g=ln_g.reshape(1, -1), ln_b=ln_b.reshape(1, -1))


def kernel(x_prompt, x_sample, cache_mla_latent, cache_mla_krope, cache_fox_k, cache_fox_v, cache_fox_logf,
           cache_mem_k, cache_mem_v, mem_prompt, w_in, b_fox_f, mla_q_norm, mla_kv_norm, w_uq, w_ukv,
           w_mem_kv, w_out, ln_g, ln_b):
    batch, seq, _ = x_prompt.shape
    dec_batch, dec_seq, _ = x_sample.shape
    depth = w_in.shape[0]
    past = cache_mla_latent.shape[2]
    kv_len = past + dec_seq
    tk_s = 768
    kv_pad = -(-kv_len // tk_s) * tk_s

    cos_p, sin_p = _rope_tables(jnp.arange(seq))
    cos_s, sin_s = _rope_tables(past + jnp.arange(dec_seq))
    zero_carry = jnp.zeros((batch, 8, LANES), F32)

    yp = x_prompt.reshape(batch * seq, D_MODEL)
    ys = x_sample.reshape(dec_batch * dec_seq, D_MODEL)
    mem2d = mem_prompt.reshape(batch * N_MEM, D_MODEL)
    p_out, s_out, p_mk, p_mv = [], [], [], []
    for l in range(depth):
        w = _layer_weights(w_in[l], b_fox_f[l], mla_q_norm[l], mla_kv_norm[l], w_uq[l], w_ukv[l],
                           w_out[l], ln_g[l], ln_b[l])
        mk, mv = _mem_kv(mem2d, w_mem_kv[l].astype(BF16))
        (lat, kr, fk, fv, logf, q, k, v, fqa, fka, fva, gmla, gfox, mmem) = _proj(
            yp, zero_carry, mk.reshape(batch, N_MEM, -1), mv.reshape(batch, N_MEM, -1), cos_p, sin_p, w,
            batch=batch, seq=seq, tm=512)
        o_mla = _attn(q, k, v, gmla, name="attn_mla", batch=batch, tq_total=seq, tk_total=seq, tq=512, tk=512,
                      dk=MLA_DK, past=0, kv_len=seq, chunk=CHUNK, ones_col=False)
        o_fox = _attn(fqa, fka, fva, gfox, name="attn_fox", batch=batch, tq_total=seq, tk_total=seq, tq=512,
                      tk=512, dk=FOX_DK, past=0, kv_len=seq, chunk=1, ones_col=True)
        yp = _out(o_mla, o_fox, mmem, yp, w, tm=512)
        p_out.append((lat, kr, fk, fv, logf))
        p_mk.append(mk)
        p_mv.append(mv)

        rows_c = dec_batch * past
        logf_c = jnp.pad(cache_fox_logf[l].reshape(rows_c, HEADS),
                         ((0, 0), (BIAS_LANE, LANES - BIAS_LANE - HEADS)))
        ck, cv, cfka, cfva, carry = _cache_prep(
            cache_mla_latent[l].reshape(rows_c, -1), cache_mla_krope[l].reshape(rows_c, -1),
            cache_fox_k[l].reshape(rows_c, -1), cache_fox_v[l].reshape(rows_c, -1), logf_c, w['wukv'],
            batch=dec_batch, seq=past, seq_pad=kv_pad, tm=256)
        (lat, kr, fk, fv, logf, q, k, v, fqa, fka, fva, gmla, gfox, mmem) = _proj(
            ys, carry, cache_mem_k[l].reshape(dec_batch, N_MEM, -1), cache_mem_v[l].reshape(dec_batch, N_MEM, -1),
            cos_s, sin_s, w, batch=dec_batch, seq=dec_seq, tm=dec_seq,
            kv_bufs=(ck, cv, cfka, cfva), kv_seq=kv_pad, kv_start=past)
        o_mla = _attn(q, k, v, gmla, name="attn_mla_s", batch=dec_batch,
                      tq_total=dec_seq, tk_total=kv_pad, tq=dec_seq, tk=tk_s, dk=MLA_DK, past=past,
                      kv_len=kv_len, chunk=CHUNK, ones_col=False)
        o_fox = _attn(fqa, fka, fva, gfox, name="attn_fox_s", batch=dec_batch,
                      tq_total=dec_seq, tk_total=kv_pad, tq=dec_seq, tk=tk_s, dk=FOX_DK, past=past,
                      kv_len=kv_len, chunk=1, ones_col=True)
        ys = _out(o_mla, o_fox, mmem, ys, w, tm=dec_batch * dec_seq)
        s_out.append((lat, kr, fk, fv, logf))

    def stack(rows, i, shape):
        return jnp.stack([r[i].reshape(shape) for r in rows], axis=0)

    pb, sb = (batch, seq), (dec_batch, dec_seq)
    return (yp.reshape(batch, seq, D_MODEL), ys.reshape(dec_batch, dec_seq, D_MODEL),
            stack(p_out, 0, pb + (MLA_KV_RANK,)), stack(p_out, 1, pb + (MLA_ROPE,)),
            stack(p_out, 2, pb + (HEADS, FOX_DIM)), stack(p_out, 3, pb + (HEADS, FOX_DIM)),
            stack(p_out, 4, pb + (HEADS,)),
            jnp.stack([a.reshape(batch, N_MEM, HEADS, MEM_DIM) for a in p_mk], axis=0),
            jnp.stack([a.reshape(batch, N_MEM, HEADS, MEM_DIM) for a in p_mv], axis=0),
            stack(s_out, 0, sb + (MLA_KV_RANK,)), stack(s_out, 1, sb + (MLA_ROPE,)),
            stack(s_out, 2, sb + (HEADS, FOX_DIM)), stack(s_out, 3, sb + (HEADS, FOX_DIM)),
            stack(s_out, 4, sb + (HEADS,)))
```

```python
import functools

import jax
import jax.numpy as jnp
from jax import lax
from jax.experimental import pallas as pl
from jax.experimental.pallas import tpu as pltpu

F32 = jnp.float32
BF16 = jnp.bfloat16

D_MODEL = 1024
CHUNK = 64
N_MEM = 256
HEADS = 4
MLA_NOPE = 128
MLA_ROPE = 64
MLA_V = 128
MLA_Q_RANK = 384
MLA_KV_RANK = 256
FOX_DIM = 64
MEM_DIM = 64
ROPE_THETA = 10000.0
NORM_EPS = 1e-6
NEG_INF = -1e30
MLA_SCALE = (MLA_NOPE + MLA_ROPE) ** -0.5
FOX_SCALE = FOX_DIM ** -0.5
MEM_SCALE = MEM_DIM ** -0.5
DEPTH = 2
DEEPNORM_ALPHA = (2 * DEPTH) ** 0.25
LOG2E = 1.4426950408889634

LANES = 128
MLA_DK = 256
FOX_DK = 128
BIAS_LANE = 64
VMEM_LIMIT = 52 * 1024 * 1024
ATTN_ROW_BLOCK = 32

O_CQ, O_CKV, O_GMLA, O_FQ, O_FK, O_FV, O_GFOX, O_MQ, O_GMEM, O_KR, O_END = (
    0, 384, 640, 1152, 1408, 1664, 1920, 2176, 2432, 2688, 2816)


def _params(sem):
    return pltpu.CompilerParams(dimension_semantics=sem, vmem_limit_bytes=VMEM_LIMIT)


def _silu(x):
    return x * jax.nn.sigmoid(x)


def _rms(x, g):
    return x * lax.rsqrt(jnp.mean(x * x, axis=-1, keepdims=True) + NORM_EPS) * g


def _rope128(x, cos, sin):
    lane = lax.broadcasted_iota(jnp.int32, x.shape, 1)
    swapped = jnp.where(lane < MLA_ROPE // 2, pltpu.roll(x, LANES - MLA_ROPE // 2, 1),
                        pltpu.roll(x, MLA_ROPE // 2, 1))
    return x * cos + swapped * sin


def _cumsum_rows(x):
    n = x.shape[0]
    row = lax.broadcasted_iota(jnp.int32, x.shape, 0)
    shift = 1
    while shift < n:
        x = x + jnp.where(row >= shift, pltpu.roll(x, shift, 0), 0.0)
        shift *= 2
    return x


def _split3(x):
    hi = x.astype(BF16).astype(F32)
    r = x - hi
    mid = r.astype(BF16).astype(F32)
    return hi, mid, r - mid


def _head_cols(x, h):
    col = x[:, (h // 2) * LANES:(h // 2 + 1) * LANES]
    return pltpu.roll(col, FOX_DIM, 1) if h % 2 else col


def _fox_kv_aug(fk_head, fv_head, cum, fka_ref, fva_ref):
    rows = cum.shape[0]
    lane = lax.broadcasted_iota(jnp.int32, (rows, LANES), 1)
    for h in range(HEADS):
        c = jnp.broadcast_to(cum[:, BIAS_LANE + h:BIAS_LANE + h + 1], (rows, LANES)) * LOG2E
        hi, mid, lo = _split3(c)
        kb = jnp.where(lane < BIAS_LANE + 3, 1.0,
             jnp.where(lane == BIAS_LANE + 3, -hi,
             jnp.where(lane == BIAS_LANE + 4, -mid,
             jnp.where(lane == BIAS_LANE + 5, -lo, 0.0))))
        ka = jnp.where(lane < BIAS_LANE, fk_head(h), kb)
        va = jnp.where(lane < BIAS_LANE, fv_head(h),
                       jnp.where(lane == BIAS_LANE, 1.0, 0.0))
        fka_ref[:, h * LANES:(h + 1) * LANES] = ka.astype(BF16)
        fva_ref[:, h * LANES:(h + 1) * LANES] = va.astype(BF16)


def _mla_kv(lat, kr128, wukv_ref, k_ref, v_ref):
    rows = lat.shape[0]
    kv = jnp.dot(lat.astype(BF16), wukv_ref[...], preferred_element_type=F32)
    lane = lax.broadcasted_iota(jnp.int32, (rows, LANES), 1)
    krb = jnp.where(lane < MLA_ROPE, kr128, 0.0).astype(BF16)
    for h in range(HEADS):
        k_ref[:, h * MLA_DK:h * MLA_DK + MLA_NOPE] = kv[:, 2 * h * LANES:(2 * h + 1) * LANES].astype(BF16)
        k_ref[:, h * MLA_DK + MLA_NOPE:(h + 1) * MLA_DK] = krb
        v_ref[:, h * MLA_V:(h + 1) * MLA_V] = kv[:, (2 * h + 1) * LANES:(2 * h + 2) * LANES].astype(BF16)


def _proj_kernel(x_ref, win_ref, wuq_ref, wukv_ref, qg_ref, kvg_ref, bf_ref, cos_ref, sin_ref,
                 cin_ref, mk_ref, mv_ref,
                 lat_ref, kr_ref, fk_ref, fv_ref, logf_ref, q_ref, k_ref, v_ref,
                 fqa_ref, fka_ref, fva_ref, gmla_ref, gfox_ref, mmem_ref, carry_ref):
    tm = x_ref.shape[0]

    @pl.when(pl.program_id(1) == 0)
    def _():
        carry_ref[...] = cin_ref[0]

    xb = x_ref[...].astype(BF16)

    def seg(lo, hi):
        return jnp.dot(xb, win_ref[:, lo:hi], preferred_element_type=F32)

    cos = cos_ref[...]
    sin = sin_ref[...]
    lane = lax.broadcasted_iota(jnp.int32, (tm, LANES), 1)

    cqn = _rms(seg(O_CQ, O_CKV), qg_ref[...])
    q = jnp.dot(cqn.astype(BF16), wuq_ref[...], preferred_element_type=F32) * (MLA_SCALE * LOG2E)
    for h in range(HEADS):
        q_ref[:, h * MLA_DK:h * MLA_DK + MLA_NOPE] = q[:, 2 * h * LANES:(2 * h + 1) * LANES].astype(BF16)
        q_ref[:, h * MLA_DK + MLA_NOPE:(h + 1) * MLA_DK] = _rope128(
            q[:, (2 * h + 1) * LANES:(2 * h + 2) * LANES], cos, sin).astype(BF16)

    lat = _rms(seg(O_CKV, O_GMLA), kvg_ref[...])
    lat_ref[...] = lat
    krl = seg(O_KR, O_END)
    kr128 = _rope128(krl, cos, sin)
    kr_ref[...] = kr128[:, :MLA_ROPE]
    _mla_kv(lat, kr128, wukv_ref, k_ref, v_ref)

    z = krl + bf_ref[...]
    logf = jnp.minimum(z, 0.0) - jnp.log1p(jnp.exp(-jnp.abs(z)))
    logf = jnp.where((lane >= BIAS_LANE) & (lane < BIAS_LANE + HEADS), logf, 0.0)
    logf_ref[...] = pltpu.roll(logf, LANES - BIAS_LANE, 1)[:, :HEADS]
    cum = _cumsum_rows(logf) + carry_ref[0:1, :]
    carry_ref[...] = jnp.broadcast_to(cum[tm - 1:tm, :], carry_ref.shape)

    fq = seg(O_FQ, O_FK)
    fk = seg(O_FK, O_FV)
    fv = seg(O_FV, O_GFOX)
    for h in range(HEADS):
        fk_ref[:, h, :] = _head_cols(fk, h)[:, :FOX_DIM]
        fv_ref[:, h, :] = _head_cols(fv, h)[:, :FOX_DIM]
    _fox_kv_aug(lambda h: _head_cols(fk, h), lambda h: _head_cols(fv, h), cum, fka_ref, fva_ref)
    gfox = _silu(seg(O_GFOX, O_MQ))
    for h in range(HEADS):
        c = jnp.broadcast_to(cum[:, BIAS_LANE + h:BIAS_LANE + h + 1], (tm, LANES)) * LOG2E
        hi, mid, lo = _split3(c)
        qb = jnp.where(lane == BIAS_LANE, hi,
             jnp.where(lane == BIAS_LANE + 1, mid,
             jnp.where(lane == BIAS_LANE + 2, lo,
             jnp.where(lane < BIAS_LANE + 6, 1.0, 0.0))))
        qa = jnp.where(lane < BIAS_LANE, _head_cols(fq, h) * (FOX_SCALE * LOG2E), qb)
        fqa_ref[:, h * LANES:(h + 1) * LANES] = qa.astype(BF16)
        gfox_ref[:, h * LANES:(h + 1) * LANES] = jnp.where(lane < BIAS_LANE, _head_cols(gfox, h), 0.0)

    gmla_ref[...] = _silu(seg(O_GMLA, O_FQ))

    mq = seg(O_MQ, O_GMEM).astype(BF16)
    gmem = _silu(seg(O_GMEM, O_KR))
    mk = mk_ref[0]
    mv = mv_ref[0].astype(BF16)
    key_head = lax.broadcasted_iota(jnp.int32, mk.shape, 1) // MEM_DIM
    out_head = lax.broadcasted_iota(jnp.int32, (tm, HEADS * MEM_DIM), 1) // MEM_DIM
    o_mem = jnp.zeros((tm, HEADS * MEM_DIM), F32)
    for h in range(HEADS):
        mk_h = jnp.where(key_head == h, mk, 0.0).astype(BF16)
        s = lax.dot_general(mq, mk_h, (((1,), (1,)), ((), ())), preferred_element_type=F32) * MEM_SCALE
        e = jnp.exp(s - jnp.max(s, axis=-1, keepdims=True))
        p = e / jnp.sum(e, axis=-1, keepdims=True)
        o_h = jnp.dot(p.astype(BF16), mv, preferred_element_type=F32)
        o_mem = jnp.where(out_head == h, o_h, o_mem)
    mmem_ref[...] = (o_mem * gmem).astype(BF16)


KV_OUTPUTS = (6, 7, 9, 10)


def _proj(x, cin, mk, mv, cos, sin, w, *, batch, seq, tm, kv_bufs=None, kv_seq=None, kv_start=0):
    rows = batch * seq
    nt = seq // tm
    row_map = lambda b, i: (b * nt + i, 0)
    full = lambda b, i: (0, 0)
    per_b = lambda b, i: (b, 0, 0)
    tab_map = lambda b, i: (i, 0)

    def rows_spec(width):
        return pl.BlockSpec((tm, width), row_map)

    def full_spec(a):
        return pl.BlockSpec(a.shape, full, pipeline_mode=pl.Buffered(1))

    out_widths = [(MLA_KV_RANK, F32), (MLA_ROPE, F32), (HEADS * FOX_DIM, F32), (HEADS * FOX_DIM, F32),
                  (HEADS, F32), (HEADS * MLA_DK, BF16), (HEADS * MLA_DK, BF16), (HEADS * MLA_V, BF16),
                  (HEADS * FOX_DK, BF16), (HEADS * FOX_DK, BF16), (HEADS * FOX_DK, BF16),
                  (HEADS * MLA_V, F32), (HEADS * FOX_DK, F32), (HEADS * MEM_DIM, BF16)]
    out_specs = [rows_spec(wd) for wd, _ in out_widths]
    out_shape = [jax.ShapeDtypeStruct((rows, wd), dt) for wd, dt in out_widths]
    for pos in (2, 3):
        out_specs[pos] = pl.BlockSpec((None, tm, HEADS, FOX_DIM), lambda b, i: (b, i, 0, 0))
        out_shape[pos] = jax.ShapeDtypeStruct((batch, seq, HEADS, FOX_DIM), F32)
    n_in = 12
    extra_in, extra_specs, aliases, kern = [], [], {}, _proj_kernel
    if kv_bufs is not None:
        kv_map = lambda b, i: (b * (kv_seq // tm) + kv_start // tm + i, 0)
        for n, (pos, buf) in enumerate(zip(KV_OUTPUTS, kv_bufs)):
            out_specs[pos] = pl.BlockSpec((tm, out_widths[pos][0]), kv_map)
            out_shape[pos] = jax.ShapeDtypeStruct(buf.shape, buf.dtype)
            aliases[n_in + n] = pos
        extra_in = list(kv_bufs)
        extra_specs = [pl.BlockSpec(memory_space=pl.ANY)] * len(kv_bufs)

        def kern(*refs):
            _proj_kernel(*refs[:n_in], *refs[n_in + len(kv_bufs):])

    return pl.pallas_call(
        kern,
        grid=(batch, nt),
        in_specs=[rows_spec(D_MODEL), full_spec(w['win']), full_spec(w['wuq']), full_spec(w['wukv']),
                  full_spec(w['qg']), full_spec(w['kvg']), full_spec(w['bf']),
                  pl.BlockSpec((tm, LANES), tab_map), pl.BlockSpec((tm, LANES), tab_map),
                  pl.BlockSpec((1, 8, LANES), per_b),
                  pl.BlockSpec((1, N_MEM, HEADS * MEM_DIM), per_b),
                  pl.BlockSpec((1, N_MEM, HEADS * MEM_DIM), per_b)] + extra_specs,
        out_specs=out_specs,
        out_shape=out_shape,
        input_output_aliases=aliases,
        scratch_shapes=[pltpu.VMEM((8, LANES), F32)],
        compiler_params=_params(("arbitrary", "arbitrary")),
        name="proj",
    )(x, w['win'], w['wuq'], w['wukv'], w['qg'], w['kvg'], w['bf'], cos, sin, cin, mk, mv, *extra_in)


def _cache_kernel(lat_ref, kr_ref, fk_ref, fv_ref, logf_ref, wukv_ref,
                  k_ref, v_ref, fka_ref, fva_ref, cout_ref, carry_ref, *, nt):
    tm = lat_ref.shape[0]
    i = pl.program_id(1)

    @pl.when(i == 0)
    def _():
        carry_ref[...] = jnp.zeros_like(carry_ref)

    @pl.when(i < nt)
    def _():
        kr128 = jnp.concatenate([kr_ref[...], jnp.zeros((tm, LANES - MLA_ROPE), F32)], axis=1)
        _mla_kv(lat_ref[...], kr128, wukv_ref, k_ref, v_ref)
        cum = _cumsum_rows(logf_ref[...]) + carry_ref[0:1, :]
        carry_ref[...] = jnp.broadcast_to(cum[tm - 1:tm, :], carry_ref.shape)
        cout_ref[0] = carry_ref[...]
        zeros = jnp.zeros((tm, LANES - FOX_DIM), F32)
        _fox_kv_aug(lambda h: jnp.concatenate([fk_ref[:, h, :], zeros], axis=1),
                    lambda h: jnp.concatenate([fv_ref[:, h, :], zeros], axis=1), cum, fka_ref, fva_ref)

    @pl.when(i >= nt)
    def _():
        for ref in (k_ref, v_ref, fka_ref, fva_ref):
            ref[...] = jnp.zeros_like(ref)


def _cache_prep(lat, kr, fk, fv, logf128, wukv, *, layer, batch, seq, seq_pad, tm):
    nt = seq // tm
    nt_pad = seq_pad // tm
    in_map = lambda b, i: (b * nt + jnp.minimum(i, nt - 1), 0)
    in_spec = lambda width: pl.BlockSpec((tm, width), in_map)
    cache_spec = lambda *minor: pl.BlockSpec(
        (None, None, tm) + minor, lambda b, i: (layer, b, jnp.minimum(i, nt - 1)) + (0,) * len(minor))
    out_spec = lambda width: pl.BlockSpec((tm, width), lambda b, i: (b * nt_pad + i, 0))
    out_widths = [HEADS * MLA_DK, HEADS * MLA_V, HEADS * FOX_DK, HEADS * FOX_DK]
    return pl.pallas_call(
        functools.partial(_cache_kernel, nt=nt),
        grid=(batch, nt_pad),
        in_specs=[cache_spec(MLA_KV_RANK), cache_spec(MLA_ROPE), cache_spec(HEADS, FOX_DIM),
                  cache_spec(HEADS, FOX_DIM), in_spec(LANES),
                  pl.BlockSpec(wukv.shape, lambda b, i: (0, 0))],
        out_specs=[out_spec(wd) for wd in out_widths]
                  + [pl.BlockSpec((1, 8, LANES), lambda b, i: (b, 0, 0))],
        out_shape=[jax.ShapeDtypeStruct((batch * seq_pad, wd), BF16) for wd in out_widths]
                  + [jax.ShapeDtypeStruct((batch, 8, LANES), F32)],
        scratch_shapes=[pltpu.VMEM((8, LANES), F32)],
        compiler_params=_params(("arbitrary", "arbitrary")),
        name="cache_prep",
    )(lat, kr, fk, fv, logf128, wukv)


def _attn_kernel(q_ref, k_ref, v_ref, g_ref, o_ref, s_ref, p_ref, m_ref, l_ref, acc_ref,
                 *, tk, dk, past, kv_len, chunk, ones_col):
    tq = q_ref.shape[0]
    dv = LANES
    rb = min(ATTN_ROW_BLOCK, tq)
    q0 = past + pl.program_id(1) * tq
    n_full = jnp.minimum((q0 // chunk + 1) * chunk, kv_len) // tk
    n_all = (jnp.minimum(((q0 + tq - 1) // chunk + 1) * chunk, kv_len) + tk - 1) // tk
    lane = lax.broadcasted_iota(jnp.int32, (rb, LANES), 1)

    m_ref[...] = jnp.full(m_ref.shape, NEG_INF, F32)
    l_ref[...] = jnp.zeros(l_ref.shape, F32)
    acc_ref[...] = jnp.zeros(acc_ref.shape, F32)

    def step(j, carry, masked):
        off = pl.multiple_of(j * tk, tk)
        for h in range(HEADS):
            s_ref[h] = lax.dot_general(q_ref[:, h * dk:(h + 1) * dk], k_ref[pl.ds(off, tk), h * dk:(h + 1) * dk],
                                       (((1,), (1,)), ((), ())), preferred_element_type=F32)
        for h in range(HEADS):
            hs = slice(h * dv, (h + 1) * dv)
            for r in range(0, tq, rb):
                rows = slice(r, r + rb)
                cols = [s_ref[h, rows, c * LANES:(c + 1) * LANES] for c in range(tk // LANES)]
                if masked:
                    qpos = q0 + r + lax.broadcasted_iota(jnp.int32, (rb, LANES), 0)
                    limit = jnp.minimum((qpos // chunk + 1) * chunk, kv_len) - off
                    cols = [jnp.where(lane + c * LANES < limit, x, NEG_INF) for c, x in enumerate(cols)]
                smax = functools.reduce(jnp.maximum, cols)
                m_old = m_ref[rows, hs]
                m_new = jnp.maximum(m_old, jnp.max(smax, axis=1, keepdims=True))
                alpha = jnp.exp2(m_old - m_new)
                ps = [jnp.exp2(x - m_new) for x in cols]
                for c, x in enumerate(ps):
                    p_ref[h, rows, c * LANES:(c + 1) * LANES] = x.astype(BF16)
                m_ref[rows, hs] = m_new
                acc_ref[rows, hs] = acc_ref[rows, hs] * alpha
                if not ones_col:
                    l_ref[rows, hs] = l_ref[rows, hs] * alpha + functools.reduce(jnp.add, ps)
            acc_ref[:, hs] += jnp.dot(p_ref[h], v_ref[pl.ds(off, tk), hs], preferred_element_type=F32)
        return carry

    lax.fori_loop(0, n_full, functools.partial(step, masked=False), 0)
    lax.fori_loop(n_full, n_all, functools.partial(step, masked=True), 0)
    for h in range(HEADS):
        hs = slice(h * dv, (h + 1) * dv)
        acc = acc_ref[:, hs]
        if ones_col:
            l = acc[:, BIAS_LANE:BIAS_LANE + 1]
        else:
            l = jnp.sum(l_ref[:, hs], axis=1, keepdims=True)
        o_ref[:, hs] = (acc / l * g_ref[:, hs]).astype(BF16)


def _attn(q, k, v, g, *, name, batch, tq_total, tk_total, tq, tk, dk, past, kv_len, chunk, ones_col):
    nq = tq_total // tq
    dv = LANES
    kern = functools.partial(_attn_kernel, tk=tk, dk=dk, past=past, kv_len=kv_len, chunk=chunk,
                             ones_col=ones_col)
    stats = pltpu.VMEM((tq, HEADS * dv), F32)
    return pl.pallas_call(
        kern,
        grid=(batch, nq),
        in_specs=[pl.BlockSpec((tq, HEADS * dk), lambda b, i: (b * nq + i, 0)),
                  pl.BlockSpec((tk_total, HEADS * dk), lambda b, i: (b, 0)),
                  pl.BlockSpec((tk_total, HEADS * dv), lambda b, i: (b, 0)),
                  pl.BlockSpec((tq, HEADS * dv), lambda b, i: (b * nq + i, 0))],
        out_specs=pl.BlockSpec((tq, HEADS * dv), lambda b, i: (b * nq + i, 0)),
        out_shape=jax.ShapeDtypeStruct((batch * tq_total, HEADS * dv), BF16),
        scratch_shapes=[pltpu.VMEM((HEADS, tq, tk), F32), pltpu.VMEM((HEADS, tq, tk), BF16),
                        stats, stats, stats],
        compiler_params=_params(("arbitrary", "arbitrary")),
        name=name,
    )(q, k, v, g)


def _out_kernel(mla_ref, fox_ref, mem_ref, x_ref, wmla_ref, wfox_ref, wmem_ref, g_ref, b_ref, y_ref):
    out = jnp.dot(mla_ref[...], wmla_ref[...], preferred_element_type=F32)
    out += jnp.dot(fox_ref[...], wfox_ref[...], preferred_element_type=F32)
    out += jnp.dot(mem_ref[...], wmem_ref[...], preferred_element_type=F32)
    z = DEEPNORM_ALPHA * x_ref[...] + out
    mu = jnp.mean(z, axis=-1, keepdims=True)
    zc = z - mu
    var = jnp.mean(zc * zc, axis=-1, keepdims=True)
    y_ref[...] = zc * lax.rsqrt(var + NORM_EPS) * g_ref[...] + b_ref[...]


def _out(mla, fox, mem, x, w, *, tm):
    rows = x.shape[0]
    rows_spec = lambda width: pl.BlockSpec((tm, width), lambda r: (r, 0))
    full_spec = lambda a: pl.BlockSpec(a.shape, lambda r: (0, 0))
    return pl.pallas_call(
        _out_kernel,
        grid=(rows // tm,),
        in_specs=[rows_spec(mla.shape[1]), rows_spec(fox.shape[1]), rows_spec(mem.shape[1]),
                  rows_spec(D_MODEL), full_spec(w['wo_mla']), full_spec(w['wo_fox']),
                  full_spec(w['wo_mem']), full_spec(w['ln_g']), full_spec(w['ln_b'])],
        out_specs=rows_spec(D_MODEL),
        out_shape=jax.ShapeDtypeStruct((rows, D_MODEL), F32),
        compiler_params=_params(("arbitrary",)),
        name="out_ln",
    )(mla, fox, mem, x, w['wo_mla'], w['wo_fox'], w['wo_mem'], w['ln_g'], w['ln_b'])


def _memkv_kernel(m_ref, w_ref, k_ref, v_ref):
    kv = jnp.dot(m_ref[...].astype(BF16), w_ref[...], preferred_element_type=F32)
    k_ref[...] = kv[:, :HEADS * MEM_DIM]
    v_ref[...] = kv[:, HEADS * MEM_DIM:]


def _mem_kv(mem, wmem):
    rows = mem.shape[0]
    width = HEADS * MEM_DIM
    return pl.pallas_call(
        _memkv_kernel,
        grid=(rows // N_MEM,),
        in_specs=[pl.BlockSpec((N_MEM, D_MODEL), lambda r: (r, 0)),
                  pl.BlockSpec(wmem.shape, lambda r: (0, 0))],
        out_specs=[pl.BlockSpec((N_MEM, width), lambda r: (r, 0))] * 2,
        out_shape=[jax.ShapeDtypeStruct((rows, width), F32)] * 2,
        compiler_params=_params(("arbitrary",)),
        name="mem_kv",
    )(mem, wmem)


def _rope_tables(pos):
    half = MLA_ROPE // 2
    inv = ROPE_THETA ** (-jnp.arange(half, dtype=F32) / half)
    ang = pos.astype(F32)[:, None] * inv[None, :]
    cos, sin = jnp.cos(ang), jnp.sin(ang)
    n = pos.shape[0]
    return (jnp.concatenate([cos, cos, jnp.ones((n, LANES - MLA_ROPE), F32)], axis=1),
            jnp.concatenate([-sin, sin, jnp.zeros((n, LANES - MLA_ROPE), F32)], axis=1))


def _layer_weights(w_in, b_f, qg, kvg, w_uq, w_ukv, w_out, ln_g, ln_b):
    c = [0]
    for n in (MLA_Q_RANK, MLA_KV_RANK, MLA_ROPE, HEADS * MLA_V, 256, 256, 256, HEADS, 256, 256, 256):
        c.append(c[-1] + n)
    s = lambda i: w_in[:, c[i]:c[i + 1]]
    win = jnp.concatenate([s(0), s(1), s(3), s(4), s(5), s(6), s(8), s(9), s(10), s(2), s(7),
                           jnp.zeros((D_MODEL, LANES - MLA_ROPE - HEADS), F32)], axis=1).astype(BF16)
    wuq = jnp.pad(w_uq.reshape(MLA_Q_RANK, HEADS, MLA_NOPE + MLA_ROPE),
                  ((0, 0), (0, 0), (0, MLA_DK - MLA_NOPE - MLA_ROPE))).reshape(MLA_Q_RANK, HEADS * MLA_DK)
    wo_fox = jnp.pad(w_out[HEADS * MLA_V:HEADS * MLA_V + HEADS * FOX_DIM].reshape(HEADS, FOX_DIM, D_MODEL),
                     ((0, 0), (0, FOX_DK - FOX_DIM), (0, 0))).reshape(HEADS * FOX_DK, D_MODEL)
    bf = jnp.zeros((1, LANES), F32).at[0, BIAS_LANE:BIAS_LANE + HEADS].set(b_f)
    return dict(win=win, wuq=wuq.astype(BF16), wukv=w_ukv.astype(BF16),
                qg=qg.reshape(1, -1), kvg=kvg.reshape(1, -1), bf=bf,
                wo_mla=w_out[:HEADS * MLA_V].astype(BF16), wo_fox=wo_fox.astype(BF16),
                wo_mem=w_out[HEADS * MLA_V + HEADS * FOX_DIM:].astype(BF16),
                ln_g=ln_g.reshape(1, -1), ln_b=ln_b.reshape(1, -1))


def kernel(x_prompt, x_sample, cache_mla_latent, cache_mla_krope, cache_fox_k, cache_fox_v, cache_fox_logf,
           cache_mem_k, cache_mem_v, mem_prompt, w_in, b_fox_f, mla_q_norm, mla_kv_norm, w_uq, w_ukv,
           w_mem_kv, w_out, ln_g, ln_b):
    batch, seq, _ = x_prompt.shape
    dec_batch, dec_seq, _ = x_sample.shape
    depth = w_in.shape[0]
    past = cache_mla_latent.shape[2]
    kv_len = past + dec_seq
    tk_s = 768
    kv_pad = -(-kv_len // tk_s) * tk_s

    cos_p, sin_p = _rope_tables(jnp.arange(seq))
    cos_s, sin_s = _rope_tables(past + jnp.arange(dec_seq))
    zero_carry = jnp.zeros((batch, 8, LANES), F32)

    yp = x_prompt.reshape(batch * seq, D_MODEL)
    ys = x_sample.reshape(dec_batch * dec_seq, D_MODEL)
    mem2d = mem_prompt.reshape(batch * N_MEM, D_MODEL)
    p_out, s_out, p_mk, p_mv = [], [], [], []
    for l in range(depth):
        w = _layer_weights(w_in[l], b_fox_f[l], mla_q_norm[l], mla_kv_norm[l], w_uq[l], w_ukv[l],
                           w_out[l], ln_g[l], ln_b[l])
        mk, mv = _mem_kv(mem2d, w_mem_kv[l].astype(BF16))
        (lat, kr, fk, fv, logf, q, k, v, fqa, fka, fva, gmla, gfox, mmem) = _proj(
            yp, zero_carry, mk.reshape(batch, N_MEM, -1), mv.reshape(batch, N_MEM, -1), cos_p, sin_p, w,
            batch=batch, seq=seq, tm=512)
        o_mla = _attn(q, k, v, gmla, name="attn_mla", batch=batch, tq_total=seq, tk_total=seq, tq=512, tk=512,
                      dk=MLA_DK, past=0, kv_len=seq, chunk=CHUNK, ones_col=False)
        o_fox = _attn(fqa, fka, fva, gfox, name="attn_fox", batch=batch, tq_total=seq, tk_total=seq, tq=512,
                      tk=512, dk=FOX_DK, past=0, kv_len=seq, chunk=1, ones_col=True)
        yp = _out(o_mla, o_fox, mmem, yp, w, tm=512)
        p_out.append((lat, kr, fk, fv, logf))
        p_mk.append(mk)
        p_mv.append(mv)

        rows_c = dec_batch * past
        logf_c = jnp.pad(cache_fox_logf[l].reshape(rows_c, HEADS),
                         ((0, 0), (BIAS_LANE, LANES - BIAS_LANE - HEADS)))
        ck, cv, cfka, cfva, carry = _cache_prep(
            cache_mla_latent, cache_mla_krope, cache_fox_k, cache_fox_v, logf_c, w['wukv'],
            layer=l, batch=dec_batch, seq=past, seq_pad=kv_pad, tm=256)
        (lat, kr, fk, fv, logf, q, k, v, fqa, fka, fva, gmla, gfox, mmem) = _proj(
            ys, carry, cache_mem_k[l].reshape(dec_batch, N_MEM, -1), cache_mem_v[l].reshape(dec_batch, N_MEM, -1),
            cos_s, sin_s, w, batch=dec_batch, seq=dec_seq, tm=dec_seq,
            kv_bufs=(ck, cv, cfka, cfva), kv_seq=kv_pad, kv_start=past)
        o_mla = _attn(q, k, v, gmla, name="attn_mla_s", batch=dec_batch,
                      tq_total=dec_seq, tk_total=kv_pad, tq=dec_seq, tk=tk_s, dk=MLA_DK, past=past,
                      kv_len=kv_len, chunk=CHUNK, ones_col=False)
        o_fox = _attn(fqa, fka, fva, gfox, name="attn_fox_s", batch=dec_batch,
                      tq_total=dec_seq, tk_total=kv_pad, tq=dec_seq, tk=tk_s, dk=FOX_DK, past=past,
                      kv_len=kv_len, chunk=1, ones_col=True)
        ys = _out(o_mla, o_fox, mmem, ys, w, tm=dec_batch * dec_seq)
        s_out.append((lat, kr, fk, fv, logf))

    def stack(rows, i, shape):
        return jnp.stack([r[i].reshape(shape) for r in rows], axis=0)

    pb, sb = (batch, seq), (dec_batch, dec_seq)
    return (yp.reshape(batch, seq, D_MODEL), ys.reshape(dec_batch, dec_seq, D_MODEL),
            stack(p_out, 0, pb + (MLA_KV_RANK,)), stack(p_out, 1, pb + (MLA_ROPE,)),
            stack(p_out, 2, pb + (HEADS, FOX_DIM)), stack(p_out, 3, pb + (HEADS, FOX_DIM)),
            stack(p_out, 4, pb + (HEADS,)),
            jnp.stack([a.reshape(batch, N_MEM, HEADS, MEM_DIM) for a in p_mk], axis=0),
            jnp.stack([a.reshape(batch, N_MEM, HEADS, MEM_DIM) for a in p_mv], axis=0),
            stack(s_out, 0, sb + (MLA_KV_RANK,)), stack(s_out, 1, sb + (MLA_ROPE,)),
            stack(s_out, 2, sb + (HEADS, FOX_DIM)), stack(s_out, 3, sb + (HEADS, FOX_DIM)),
            stack(s_out, 4, sb + (HEADS,)))
```

```python
import functools

import jax
import jax.numpy as jnp
from jax import lax
from jax.experimental import pallas as pl
from jax.experimental.pallas import tpu as pltpu

F32 = jnp.float32
BF16 = jnp.bfloat16

D_MODEL = 1024
CHUNK = 64
N_MEM = 256
HEADS = 4
MLA_NOPE = 128
MLA_ROPE = 64
MLA_V = 128
MLA_Q_RANK = 384
MLA_KV_RANK = 256
FOX_DIM = 64
MEM_DIM = 64
ROPE_THETA = 10000.0
NORM_EPS = 1e-6
NEG_INF = -1e30
MLA_SCALE = (MLA_NOPE + MLA_ROPE) ** -0.5
FOX_SCALE = FOX_DIM ** -0.5
MEM_SCALE = MEM_DIM ** -0.5
DEPTH = 2
DEEPNORM_ALPHA = (2 * DEPTH) ** 0.25
LOG2E = 1.4426950408889634

LANES = 128
MLA_DK = 256
FOX_DK = 128
BIAS_LANE = 64
VMEM_LIMIT = 52 * 1024 * 1024
ATTN_ROW_BLOCK = 32
ATTN_TQ = 512
ATTN_TK = 512
ROW_TM = 512
CACHE_TM = 256

O_CQ, O_CKV, O_GMLA, O_FQ, O_FK, O_FV, O_GFOX, O_MQ, O_GMEM, O_KR, O_END = (
    0, 384, 640, 1152, 1408, 1664, 1920, 2176, 2432, 2688, 2816)


def _params(sem):
    return pltpu.CompilerParams(dimension_semantics=sem, vmem_limit_bytes=VMEM_LIMIT)


def _silu(x):
    return x * jax.nn.sigmoid(x)


def _rms(x, g):
    return x * lax.rsqrt(jnp.mean(x * x, axis=-1, keepdims=True) + NORM_EPS) * g


def _rope128(x, cos, sin):
    lane = lax.broadcasted_iota(jnp.int32, x.shape, 1)
    swapped = jnp.where(lane < MLA_ROPE // 2, pltpu.roll(x, LANES - MLA_ROPE // 2, 1),
                        pltpu.roll(x, MLA_ROPE // 2, 1))
    return x * cos + swapped * sin


def _cumsum_rows(x):
    n = x.shape[0]
    row = lax.broadcasted_iota(jnp.int32, x.shape, 0)
    shift = 1
    while shift < n:
        x = x + jnp.where(row >= shift, pltpu.roll(x, shift, 0), 0.0)
        shift *= 2
    return x


def _split3(x):
    hi = x.astype(BF16).astype(F32)
    r = x - hi
    mid = r.astype(BF16).astype(F32)
    return hi, mid, r - mid


def _head_cols(x, h):
    col = x[:, (h // 2) * LANES:(h // 2 + 1) * LANES]
    return pltpu.roll(col, FOX_DIM, 1) if h % 2 else col


def _fox_kv_aug(fk_head, fv_head, cum, fka_ref, fva_ref):
    rows = cum.shape[0]
    lane = lax.broadcasted_iota(jnp.int32, (rows, LANES), 1)
    for h in range(HEADS):
        c = jnp.broadcast_to(cum[:, BIAS_LANE + h:BIAS_LANE + h + 1], (rows, LANES)) * LOG2E
        hi, mid, lo = _split3(c)
        kb = jnp.where(lane < BIAS_LANE + 3, 1.0,
             jnp.where(lane == BIAS_LANE + 3, -hi,
             jnp.where(lane == BIAS_LANE + 4, -mid,
             jnp.where(lane == BIAS_LANE + 5, -lo, 0.0))))
        ka = jnp.where(lane < BIAS_LANE, fk_head(h), kb)
        va = jnp.where(lane < BIAS_LANE, fv_head(h),
                       jnp.where(lane == BIAS_LANE, 1.0, 0.0))
        fka_ref[:, h * LANES:(h + 1) * LANES] = ka.astype(BF16)
        fva_ref[:, h * LANES:(h + 1) * LANES] = va.astype(BF16)


def _mla_kv(lat, kr128, wukv_ref, k_ref, v_ref):
    rows = lat.shape[0]
    kv = jnp.dot(lat.astype(BF16), wukv_ref[...], preferred_element_type=F32)
    lane = lax.broadcasted_iota(jnp.int32, (rows, LANES), 1)
    krb = jnp.where(lane < MLA_ROPE, kr128, 0.0).astype(BF16)
    for h in range(HEADS):
        k_ref[:, h * MLA_DK:h * MLA_DK + MLA_NOPE] = kv[:, 2 * h * LANES:(2 * h + 1) * LANES].astype(BF16)
        k_ref[:, h * MLA_DK + MLA_NOPE:(h + 1) * MLA_DK] = krb
        v_ref[:, h * MLA_V:(h + 1) * MLA_V] = kv[:, (2 * h + 1) * LANES:(2 * h + 2) * LANES].astype(BF16)


def _proj_kernel(x_ref, win_ref, wuq_ref, wukv_ref, qg_ref, kvg_ref, bf_ref, cos_ref, sin_ref,
                 cin_ref, mk_ref, mv_ref,
                 lat_ref, kr_ref, fk_ref, fv_ref, logf_ref, q_ref, k_ref, v_ref,
                 fqa_ref, fka_ref, fva_ref, gmla_ref, gfox_ref, mmem_ref, carry_ref):
    tm = x_ref.shape[0]

    @pl.when(pl.program_id(1) == 0)
    def _():
        carry_ref[...] = cin_ref[0]

    xb = x_ref[...].astype(BF16)

    def seg(lo, hi):
        return jnp.dot(xb, win_ref[:, lo:hi], preferred_element_type=F32)

    cos = cos_ref[...]
    sin = sin_ref[...]
    lane = lax.broadcasted_iota(jnp.int32, (tm, LANES), 1)

    cqn = _rms(seg(O_CQ, O_CKV), qg_ref[...])
    q = jnp.dot(cqn.astype(BF16), wuq_ref[...], preferred_element_type=F32) * (MLA_SCALE * LOG2E)
    for h in range(HEADS):
        q_ref[:, h * MLA_DK:h * MLA_DK + MLA_NOPE] = q[:, 2 * h * LANES:(2 * h + 1) * LANES].astype(BF16)
        q_ref[:, h * MLA_DK + MLA_NOPE:(h + 1) * MLA_DK] = _rope128(
            q[:, (2 * h + 1) * LANES:(2 * h + 2) * LANES], cos, sin).astype(BF16)

    lat = _rms(seg(O_CKV, O_GMLA), kvg_ref[...])
    lat_ref[...] = lat
    krl = seg(O_KR, O_END)
    kr128 = _rope128(krl, cos, sin)
    kr_ref[...] = kr128[:, :MLA_ROPE]
    _mla_kv(lat, kr128, wukv_ref, k_ref, v_ref)

    z = krl + bf_ref[...]
    logf = jnp.minimum(z, 0.0) - jnp.log1p(jnp.exp(-jnp.abs(z)))
    logf = jnp.where((lane >= BIAS_LANE) & (lane < BIAS_LANE + HEADS), logf, 0.0)
    logf_ref[...] = pltpu.roll(logf, LANES - BIAS_LANE, 1)[:, :HEADS]
    cum = _cumsum_rows(logf) + carry_ref[0:1, :]
    carry_ref[...] = jnp.broadcast_to(cum[tm - 1:tm, :], carry_ref.shape)

    fq = seg(O_FQ, O_FK)
    fk = seg(O_FK, O_FV)
    fv = seg(O_FV, O_GFOX)
    for h in range(HEADS):
        fk_ref[:, h, :] = _head_cols(fk, h)[:, :FOX_DIM]
        fv_ref[:, h, :] = _head_cols(fv, h)[:, :FOX_DIM]
    _fox_kv_aug(lambda h: _head_cols(fk, h), lambda h: _head_cols(fv, h), cum, fka_ref, fva_ref)
    gfox = _silu(seg(O_GFOX, O_MQ))
    for h in range(HEADS):
        c = jnp.broadcast_to(cum[:, BIAS_LANE + h:BIAS_LANE + h + 1], (tm, LANES)) * LOG2E
        hi, mid, lo = _split3(c)
        qb = jnp.where(lane == BIAS_LANE, hi,
             jnp.where(lane == BIAS_LANE + 1, mid,
             jnp.where(lane == BIAS_LANE + 2, lo,
             jnp.where(lane < BIAS_LANE + 6, 1.0, 0.0))))
        qa = jnp.where(lane < BIAS_LANE, _head_cols(fq, h) * (FOX_SCALE * LOG2E), qb)
        fqa_ref[:, h * LANES:(h + 1) * LANES] = qa.astype(BF16)
        gfox_ref[:, h * LANES:(h + 1) * LANES] = jnp.where(lane < BIAS_LANE, _head_cols(gfox, h), 0.0)

    gmla_ref[...] = _silu(seg(O_GMLA, O_FQ))

    mq = seg(O_MQ, O_GMEM).astype(BF16)
    gmem = _silu(seg(O_GMEM, O_KR))
    mk = mk_ref[0]
    mv = mv_ref[0].astype(BF16)
    key_head = lax.broadcasted_iota(jnp.int32, mk.shape, 1) // MEM_DIM
    out_head = lax.broadcasted_iota(jnp.int32, (tm, HEADS * MEM_DIM), 1) // MEM_DIM
    o_mem = jnp.zeros((tm, HEADS * MEM_DIM), F32)
    for h in range(HEADS):
        mk_h = jnp.where(key_head == h, mk, 0.0).astype(BF16)
        s = lax.dot_general(mq, mk_h, (((1,), (1,)), ((), ())), preferred_element_type=F32) * MEM_SCALE
        e = jnp.exp(s - jnp.max(s, axis=-1, keepdims=True))
        p = e / jnp.sum(e, axis=-1, keepdims=True)
        o_h = jnp.dot(p.astype(BF16), mv, preferred_element_type=F32)
        o_mem = jnp.where(out_head == h, o_h, o_mem)
    mmem_ref[...] = (o_mem * gmem).astype(BF16)


KV_OUTPUTS = (6, 7, 9, 10)


def _proj(x, cin, mk, mv, cos, sin, w, *, batch, seq, tm, kv_bufs=None, kv_seq=None, kv_start=0):
    rows = batch * seq
    nt = seq // tm
    row_map = lambda b, i: (b * nt + i, 0)
    full = lambda b, i: (0, 0)
    per_b = lambda b, i: (b, 0, 0)
    tab_map = lambda b, i: (i, 0)

    def rows_spec(width):
        return pl.BlockSpec((tm, width), row_map)

    def full_spec(a):
        return pl.BlockSpec(a.shape, full, pipeline_mode=pl.Buffered(1))

    out_widths = [(MLA_KV_RANK, F32), (MLA_ROPE, F32), (HEADS * FOX_DIM, F32), (HEADS * FOX_DIM, F32),
                  (HEADS, F32), (HEADS * MLA_DK, BF16), (HEADS * MLA_DK, BF16), (HEADS * MLA_V, BF16),
                  (HEADS * FOX_DK, BF16), (HEADS * FOX_DK, BF16), (HEADS * FOX_DK, BF16),
                  (HEADS * MLA_V, F32), (HEADS * FOX_DK, F32), (HEADS * MEM_DIM, BF16)]
    out_specs = [rows_spec(wd) for wd, _ in out_widths]
    out_shape = [jax.ShapeDtypeStruct((rows, wd), dt) for wd, dt in out_widths]
    for pos in (2, 3):
        out_specs[pos] = pl.BlockSpec((None, tm, HEADS, FOX_DIM), lambda b, i: (b, i, 0, 0))
        out_shape[pos] = jax.ShapeDtypeStruct((batch, seq, HEADS, FOX_DIM), F32)
    n_in = 12
    extra_in, extra_specs, aliases, kern = [], [], {}, _proj_kernel
    if kv_bufs is not None:
        kv_map = lambda b, i: (b * (kv_seq // tm) + kv_start // tm + i, 0)
        for n, (pos, buf) in enumerate(zip(KV_OUTPUTS, kv_bufs)):
            out_specs[pos] = pl.BlockSpec((tm, out_widths[pos][0]), kv_map)
            out_shape[pos] = jax.ShapeDtypeStruct(buf.shape, buf.dtype)
            aliases[n_in + n] = pos
        extra_in = list(kv_bufs)
        extra_specs = [pl.BlockSpec(memory_space=pl.ANY)] * len(kv_bufs)

        def kern(*refs):
            _proj_kernel(*refs[:n_in], *refs[n_in + len(kv_bufs):])

    return pl.pallas_call(
        kern,
        grid=(batch, nt),
        in_specs=[rows_spec(D_MODEL), full_spec(w['win']), full_spec(w['wuq']), full_spec(w['wukv']),
                  full_spec(w['qg']), full_spec(w['kvg']), full_spec(w['bf']),
                  pl.BlockSpec((tm, LANES), tab_map), pl.BlockSpec((tm, LANES), tab_map),
                  pl.BlockSpec((1, 8, LANES), per_b),
                  pl.BlockSpec((1, N_MEM, HEADS * MEM_DIM), per_b),
                  pl.BlockSpec((1, N_MEM, HEADS * MEM_DIM), per_b)] + extra_specs,
        out_specs=out_specs,
        out_shape=out_shape,
        input_output_aliases=aliases,
        scratch_shapes=[pltpu.VMEM((8, LANES), F32)],
        compiler_params=_params(("arbitrary", "arbitrary")),
        name="proj",
    )(x, w['win'], w['wuq'], w['wukv'], w['qg'], w['kvg'], w['bf'], cos, sin, cin, mk, mv, *extra_in)


def _cache_kernel(lat_ref, kr_ref, fk_ref, fv_ref, logf_ref, wukv_ref,
                  k_ref, v_ref, fka_ref, fva_ref, cout_ref, carry_ref, *, nt):
    tm = lat_ref.shape[0]
    i = pl.program_id(1)

    @pl.when(i == 0)
    def _():
        carry_ref[...] = jnp.zeros_like(carry_ref)

    @pl.when(i < nt)
    def _():
        kr128 = jnp.concatenate([kr_ref[...], jnp.zeros((tm, LANES - MLA_ROPE), F32)], axis=1)
        _mla_kv(lat_ref[...], kr128, wukv_ref, k_ref, v_ref)
        cum = _cumsum_rows(logf_ref[...]) + carry_ref[0:1, :]
        carry_ref[...] = jnp.broadcast_to(cum[tm - 1:tm, :], carry_ref.shape)
        cout_ref[0] = carry_ref[...]
        zeros = jnp.zeros((tm, LANES - FOX_DIM), F32)
        _fox_kv_aug(lambda h: jnp.concatenate([fk_ref[:, h, :], zeros], axis=1),
                    lambda h: jnp.concatenate([fv_ref[:, h, :], zeros], axis=1), cum, fka_ref, fva_ref)

    @pl.when(i >= nt)
    def _():
        for ref in (k_ref, v_ref, fka_ref, fva_ref):
            ref[...] = jnp.zeros_like(ref)


def _cache_prep(lat, kr, fk, fv, logf128, wukv, *, layer, batch, seq, seq_pad, tm):
    nt = seq // tm
    nt_pad = seq_pad // tm
    in_map = lambda b, i: (b * nt + jnp.minimum(i, nt - 1), 0)
    in_spec = lambda width: pl.BlockSpec((tm, width), in_map)
    cache_spec = lambda *minor: pl.BlockSpec(
        (None, None, tm) + minor, lambda b, i: (layer, b, jnp.minimum(i, nt - 1)) + (0,) * len(minor))
    out_spec = lambda width: pl.BlockSpec((tm, width), lambda b, i: (b * nt_pad + i, 0))
    out_widths = [HEADS * MLA_DK, HEADS * MLA_V, HEADS * FOX_DK, HEADS * FOX_DK]
    return pl.pallas_call(
        functools.partial(_cache_kernel, nt=nt),
        grid=(batch, nt_pad),
        in_specs=[cache_spec(MLA_KV_RANK), cache_spec(MLA_ROPE), cache_spec(HEADS, FOX_DIM),
                  cache_spec(HEADS, FOX_DIM), in_spec(LANES),
                  pl.BlockSpec(wukv.shape, lambda b, i: (0, 0))],
        out_specs=[out_spec(wd) for wd in out_widths]
                  + [pl.BlockSpec((1, 8, LANES), lambda b, i: (b, 0, 0))],
        out_shape=[jax.ShapeDtypeStruct((batch * seq_pad, wd), BF16) for wd in out_widths]
                  + [jax.ShapeDtypeStruct((batch, 8, LANES), F32)],
        scratch_shapes=[pltpu.VMEM((8, LANES), F32)],
        compiler_params=_params(("arbitrary", "arbitrary")),
        name="cache_prep",
    )(lat, kr, fk, fv, logf128, wukv)


def _attn_kernel(q_ref, k_ref, v_ref, g_ref, o_ref, s_ref, p_ref, m_ref, l_ref, acc_ref,
                 *, tk, dk, past, kv_len, chunk, ones_col):
    tq = q_ref.shape[0]
    dv = LANES
    rb = min(ATTN_ROW_BLOCK, tq)
    q0 = past + pl.program_id(1) * tq
    n_full = jnp.minimum((q0 // chunk + 1) * chunk, kv_len) // tk
    n_all = (jnp.minimum(((q0 + tq - 1) // chunk + 1) * chunk, kv_len) + tk - 1) // tk
    lane = lax.broadcasted_iota(jnp.int32, (rb, LANES), 1)

    m_ref[...] = jnp.full(m_ref.shape, NEG_INF, F32)
    l_ref[...] = jnp.zeros(l_ref.shape, F32)
    acc_ref[...] = jnp.zeros(acc_ref.shape, F32)

    def step(j, carry, masked):
        off = pl.multiple_of(j * tk, tk)
        for h in range(HEADS):
            s_ref[h] = lax.dot_general(q_ref[:, h * dk:(h + 1) * dk], k_ref[pl.ds(off, tk), h * dk:(h + 1) * dk],
                                       (((1,), (1,)), ((), ())), preferred_element_type=F32)
        for h in range(HEADS):
            hs = slice(h * dv, (h + 1) * dv)
            for r in range(0, tq, rb):
                rows = slice(r, r + rb)
                cols = [s_ref[h, rows, c * LANES:(c + 1) * LANES] for c in range(tk // LANES)]
                if masked:
                    qpos = q0 + r + lax.broadcasted_iota(jnp.int32, (rb, LANES), 0)
                    limit = jnp.minimum((qpos // chunk + 1) * chunk, kv_len) - off
                    cols = [jnp.where(lane + c * LANES < limit, x, NEG_INF) for c, x in enumerate(cols)]
                smax = functools.reduce(jnp.maximum, cols)
                m_old = m_ref[rows, hs]
                m_new = jnp.maximum(m_old, jnp.max(smax, axis=1, keepdims=True))
                alpha = jnp.exp2(m_old - m_new)
                ps = [jnp.exp2(x - m_new) for x in cols]
                for c, x in enumerate(ps):
                    p_ref[h, rows, c * LANES:(c + 1) * LANES] = x.astype(BF16)
                m_ref[rows, hs] = m_new
                acc_ref[rows, hs] = acc_ref[rows, hs] * alpha
                if not ones_col:
                    l_ref[rows, hs] = l_ref[rows, hs] * alpha + functools.reduce(jnp.add, ps)
            acc_ref[:, hs] += jnp.dot(p_ref[h], v_ref[pl.ds(off, tk), hs], preferred_element_type=F32)
        return carry

    lax.fori_loop(0, n_full, functools.partial(step, masked=False), 0)
    lax.fori_loop(n_full, n_all, functools.partial(step, masked=True), 0)
    for h in range(HEADS):
        hs = slice(h * dv, (h + 1) * dv)
        acc = acc_ref[:, hs]
        if ones_col:
            l = acc[:, BIAS_LANE:BIAS_LANE + 1]
        else:
            l = jnp.sum(l_ref[:, hs], axis=1, keepdims=True)
        o_ref[:, hs] = (acc / l * g_ref[:, hs]).astype(BF16)


def _attn(q, k, v, g, *, name, batch, tq_total, tk_total, tq, tk, dk, past, kv_len, chunk, ones_col):
    nq = tq_total // tq
    dv = LANES
    kern = functools.partial(_attn_kernel, tk=tk, dk=dk, past=past, kv_len=kv_len, chunk=chunk,
                             ones_col=ones_col)
    stats = pltpu.VMEM((tq, HEADS * dv), F32)
    return pl.pallas_call(
        kern,
        grid=(batch, nq),
        in_specs=[pl.BlockSpec((tq, HEADS * dk), lambda b, i: (b * nq + i, 0)),
                  pl.BlockSpec((tk_total, HEADS * dk), lambda b, i: (b, 0)),
                  pl.BlockSpec((tk_total, HEADS * dv), lambda b, i: (b, 0)),
                  pl.BlockSpec((tq, HEADS * dv), lambda b, i: (b * nq + i, 0))],
        out_specs=pl.BlockSpec((tq, HEADS * dv), lambda b, i: (b * nq + i, 0)),
        out_shape=jax.ShapeDtypeStruct((batch * tq_total, HEADS * dv), BF16),
        scratch_shapes=[pltpu.VMEM((HEADS, tq, tk), F32), pltpu.VMEM((HEADS, tq, tk), BF16),
                        stats, stats, stats],
        compiler_params=_params(("arbitrary", "arbitrary")),
        name=name,
    )(q, k, v, g)


def _out_kernel(mla_ref, fox_ref, mem_ref, x_ref, wmla_ref, wfox_ref, wmem_ref, g_ref, b_ref, y_ref):
    out = jnp.dot(mla_ref[...], wmla_ref[...], preferred_element_type=F32)
    out += jnp.dot(fox_ref[...], wfox_ref[...], preferred_element_type=F32)
    out += jnp.dot(mem_ref[...], wmem_ref[...], preferred_element_type=F32)
    z = DEEPNORM_ALPHA * x_ref[...] + out
    mu = jnp.mean(z, axis=-1, keepdims=True)
    zc = z - mu
    var = jnp.mean(zc * zc, axis=-1, keepdims=True)
    y_ref[...] = zc * lax.rsqrt(var + NORM_EPS) * g_ref[...] + b_ref[...]


def _out(mla, fox, mem, x, w, *, tm):
    rows = x.shape[0]
    rows_spec = lambda width: pl.BlockSpec((tm, width), lambda r: (r, 0))
    full_spec = lambda a: pl.BlockSpec(a.shape, lambda r: (0, 0))
    return pl.pallas_call(
        _out_kernel,
        grid=(rows // tm,),
        in_specs=[rows_spec(mla.shape[1]), rows_spec(fox.shape[1]), rows_spec(mem.shape[1]),
                  rows_spec(D_MODEL), full_spec(w['wo_mla']), full_spec(w['wo_fox']),
                  full_spec(w['wo_mem']), full_spec(w['ln_g']), full_spec(w['ln_b'])],
        out_specs=rows_spec(D_MODEL),
        out_shape=jax.ShapeDtypeStruct((rows, D_MODEL), F32),
        compiler_params=_params(("arbitrary",)),
        name="out_ln",
    )(mla, fox, mem, x, w['wo_mla'], w['wo_fox'], w['wo_mem'], w['ln_g'], w['ln_b'])


def _memkv_kernel(m_ref, w_ref, k_ref, v_ref):
    kv = jnp.dot(m_ref[...].astype(BF16), w_ref[...], preferred_element_type=F32)
    k_ref[...] = kv[:, :HEADS * MEM_DIM]
    v_ref[...] = kv[:, HEADS * MEM_DIM:]


def _mem_kv(mem, wmem):
    rows = mem.shape[0]
    width = HEADS * MEM_DIM
    return pl.pallas_call(
        _memkv_kernel,
        grid=(rows // N_MEM,),
        in_specs=[pl.BlockSpec((N_MEM, D_MODEL), lambda r: (r, 0)),
                  pl.BlockSpec(wmem.shape, lambda r: (0, 0))],
        out_specs=[pl.BlockSpec((N_MEM, width), lambda r: (r, 0))] * 2,
        out_shape=[jax.ShapeDtypeStruct((rows, width), F32)] * 2,
        compiler_params=_params(("arbitrary",)),
        name="mem_kv",
    )(mem, wmem)


def _rope_tables(pos):
    half = MLA_ROPE // 2
    inv = ROPE_THETA ** (-jnp.arange(half, dtype=F32) / half)
    ang = pos.astype(F32)[:, None] * inv[None, :]
    cos, sin = jnp.cos(ang), jnp.sin(ang)
    n = pos.shape[0]
    return (jnp.concatenate([cos, cos, jnp.ones((n, LANES - MLA_ROPE), F32)], axis=1),
            jnp.concatenate([-sin, sin, jnp.zeros((n, LANES - MLA_ROPE), F32)], axis=1))


def _layer_weights(w_in, b_f, qg, kvg, w_uq, w_ukv, w_out, ln_g, ln_b):
    c = [0]
    for n in (MLA_Q_RANK, MLA_KV_RANK, MLA_ROPE, HEADS * MLA_V, 256, 256, 256, HEADS, 256, 256, 256):
        c.append(c[-1] + n)
    s = lambda i: w_in[:, c[i]:c[i + 1]]
    win = jnp.concatenate([s(0), s(1), s(3), s(4), s(5), s(6), s(8), s(9), s(10), s(2), s(7),
                           jnp.zeros((D_MODEL, LANES - MLA_ROPE - HEADS), F32)], axis=1).astype(BF16)
    wuq = jnp.pad(w_uq.reshape(MLA_Q_RANK, HEADS, MLA_NOPE + MLA_ROPE),
                  ((0, 0), (0, 0), (0, MLA_DK - MLA_NOPE - MLA_ROPE))).reshape(MLA_Q_RANK, HEADS * MLA_DK)
    wo_fox = jnp.pad(w_out[HEADS * MLA_V:HEADS * MLA_V + HEADS * FOX_DIM].reshape(HEADS, FOX_DIM, D_MODEL),
                     ((0, 0), (0, FOX_DK - FOX_DIM), (0, 0))).reshape(HEADS * FOX_DK, D_MODEL)
    bf = jnp.zeros((1, LANES), F32).at[0, BIAS_LANE:BIAS_LANE + HEADS].set(b_f)
    return dict(win=win, wuq=wuq.astype(BF16), wukv=w_ukv.astype(BF16),
                qg=qg.reshape(1, -1), kvg=kvg.reshape(1, -1), bf=bf,
                wo_mla=w_out[:HEADS * MLA_V].astype(BF16), wo_fox=wo_fox.astype(BF16),
                wo_mem=w_out[HEADS * MLA_V + HEADS * FOX_DIM:].astype(BF16),
                ln_g=ln_g.reshape(1, -1), ln_b=ln_b.reshape(1, -1))


def kernel(x_prompt, x_sample, cache_mla_latent, cache_mla_krope, cache_fox_k, cache_fox_v, cache_fox_logf,
           cache_mem_k, cache_mem_v, mem_prompt, w_in, b_fox_f, mla_q_norm, mla_kv_norm, w_uq, w_ukv,
           w_mem_kv, w_out, ln_g, ln_b):
    batch, seq, _ = x_prompt.shape
    dec_batch, dec_seq, _ = x_sample.shape
    depth = w_in.shape[0]
    past = cache_mla_latent.shape[2]
    kv_len = past + dec_seq
    kv_pad = -(-kv_len // CACHE_TM) * CACHE_TM
    tk_s = kv_pad

    cos_p, sin_p = _rope_tables(jnp.arange(seq))
    cos_s, sin_s = _rope_tables(past + jnp.arange(dec_seq))
    zero_carry = jnp.zeros((batch, 8, LANES), F32)

    yp = x_prompt.reshape(batch * seq, D_MODEL)
    ys = x_sample.reshape(dec_batch * dec_seq, D_MODEL)
    mem2d = mem_prompt.reshape(batch * N_MEM, D_MODEL)
    p_out, s_out, p_mk, p_mv = [], [], [], []
    for l in range(depth):
        w = _layer_weights(w_in[l], b_fox_f[l], mla_q_norm[l], mla_kv_norm[l], w_uq[l], w_ukv[l],
                           w_out[l], ln_g[l], ln_b[l])
        mk, mv = _mem_kv(mem2d, w_mem_kv[l].astype(BF16))
        (lat, kr, fk, fv, logf, q, k, v, fqa, fka, fva, gmla, gfox, mmem) = _proj(
            yp, zero_carry, mk.reshape(batch, N_MEM, -1), mv.reshape(batch, N_MEM, -1), cos_p, sin_p, w,
            batch=batch, seq=seq, tm=ROW_TM)
        o_mla = _attn(q, k, v, gmla, name="attn_mla", batch=batch, tq_total=seq, tk_total=seq, tq=ATTN_TQ,
                      tk=ATTN_TK, dk=MLA_DK, past=0, kv_len=seq, chunk=CHUNK, ones_col=False)
        o_fox = _attn(fqa, fka, fva, gfox, name="attn_fox", batch=batch, tq_total=seq, tk_total=seq, tq=ATTN_TQ,
                      tk=ATTN_TK, dk=FOX_DK, past=0, kv_len=seq, chunk=1, ones_col=True)
        yp = _out(o_mla, o_fox, mmem, yp, w, tm=ROW_TM)
        p_out.append((lat, kr, fk, fv, logf))
        p_mk.append(mk)
        p_mv.append(mv)

        rows_c = dec_batch * past
        logf_c = jnp.pad(cache_fox_logf[l].reshape(rows_c, HEADS),
                         ((0, 0), (BIAS_LANE, LANES - BIAS_LANE - HEADS)))
        ck, cv, cfka, cfva, carry = _cache_prep(
            cache_mla_latent, cache_mla_krope, cache_fox_k, cache_fox_v, logf_c, w['wukv'],
            layer=l, batch=dec_batch, seq=past, seq_pad=kv_pad, tm=CACHE_TM)
        (lat, kr, fk, fv, logf, q, k, v, fqa, fka, fva, gmla, gfox, mmem) = _proj(
            ys, carry, cache_mem_k[l].reshape(dec_batch, N_MEM, -1), cache_mem_v[l].reshape(dec_batch, N_MEM, -1),
            cos_s, sin_s, w, batch=dec_batch, seq=dec_seq, tm=dec_seq,
            kv_bufs=(ck, cv, cfka, cfva), kv_seq=kv_pad, kv_start=past)
        o_mla = _attn(q, k, v, gmla, name="attn_mla_s", batch=dec_batch,
                      tq_total=dec_seq, tk_total=kv_pad, tq=dec_seq, tk=tk_s, dk=MLA_DK, past=past,
                      kv_len=kv_len, chunk=CHUNK, ones_col=False)
        o_fox = _attn(fqa, fka, fva, gfox, name="attn_fox_s", batch=dec_batch,
                      tq_total=dec_seq, tk_total=kv_pad, tq=dec_seq, tk=tk_s, dk=FOX_DK, past=past,
                      kv_len=kv_len, chunk=1, ones_col=True)
        ys = _out(o_mla, o_fox, mmem, ys, w, tm=dec_batch * dec_seq)
        s_out.append((lat, kr, fk, fv, logf))

    def stack(rows, i, shape):
        return jnp.stack([r[i].reshape(shape) for r in rows], axis=0)

    pb, sb = (batch, seq), (dec_batch, dec_seq)
    return (yp.reshape(batch, seq, D_MODEL), ys.reshape(dec_batch, dec_seq, D_MODEL),
            stack(p_out, 0, pb + (MLA_KV_RANK,)), stack(p_out, 1, pb + (MLA_ROPE,)),
            stack(p_out, 2, pb + (HEADS, FOX_DIM)), stack(p_out, 3, pb + (HEADS, FOX_DIM)),
            stack(p_out, 4, pb + (HEADS,)),
            jnp.stack([a.reshape(batch, N_MEM, HEADS, MEM_DIM) for a in p_mk], axis=0),
            jnp.stack([a.reshape(batch, N_MEM, HEADS, MEM_DIM) for a in p_mv], axis=0),
            stack(s_out, 0, sb + (MLA_KV_RANK,)), stack(s_out, 1, sb + (MLA_ROPE,)),
            stack(s_out, 2, sb + (HEADS, FOX_DIM)), stack(s_out, 3, sb + (HEADS, FOX_DIM)),
            stack(s_out, 4, sb + (HEADS,)))
```

```python
import functools

import jax
import jax.numpy as jnp
from jax import lax
from jax.experimental import pallas as pl
from jax.experimental.pallas import tpu as pltpu

F32 = jnp.float32
BF16 = jnp.bfloat16

D_MODEL = 1024
CHUNK = 64
N_MEM = 256
HEADS = 4
MLA_NOPE = 128
MLA_ROPE = 64
MLA_V = 128
MLA_Q_RANK = 384
MLA_KV_RANK = 256
FOX_DIM = 64
MEM_DIM = 64
ROPE_THETA = 10000.0
NORM_EPS = 1e-6
NEG_INF = -1e30
MLA_SCALE = (MLA_NOPE + MLA_ROPE) ** -0.5
FOX_SCALE = FOX_DIM ** -0.5
MEM_SCALE = MEM_DIM ** -0.5
DEPTH = 2
DEEPNORM_ALPHA = (2 * DEPTH) ** 0.25
LOG2E = 1.4426950408889634

LANES = 128
MLA_DK = 256
FOX_DK = 128
BIAS_LANE = 64
VMEM_LIMIT = 52 * 1024 * 1024
ATTN_ROW_BLOCK = 32
ATTN_TQ = 512
ATTN_TK = 512
ROW_TM = 512
CACHE_TM = 256

O_CQ, O_CKV, O_GMLA, O_FQ, O_FK, O_FV, O_GFOX, O_MQ, O_GMEM, O_KR, O_END = (
    0, 384, 640, 1152, 1408, 1664, 1920, 2176, 2432, 2688, 2816)


def _params(sem):
    return pltpu.CompilerParams(dimension_semantics=sem, vmem_limit_bytes=VMEM_LIMIT)


def _silu(x):
    return x * jax.nn.sigmoid(x)


def _rms(x, g):
    return x * lax.rsqrt(jnp.mean(x * x, axis=-1, keepdims=True) + NORM_EPS) * g


def _rope128(x, cos, sin):
    lane = lax.broadcasted_iota(jnp.int32, x.shape, 1)
    swapped = jnp.where(lane < MLA_ROPE // 2, pltpu.roll(x, LANES - MLA_ROPE // 2, 1),
                        pltpu.roll(x, MLA_ROPE // 2, 1))
    return x * cos + swapped * sin


def _cumsum_rows(x):
    n = x.shape[0]
    row = lax.broadcasted_iota(jnp.int32, x.shape, 0)
    shift = 1
    while shift < n:
        x = x + jnp.where(row >= shift, pltpu.roll(x, shift, 0), 0.0)
        shift *= 2
    return x


def _split3(x):
    hi = x.astype(BF16).astype(F32)
    r = x - hi
    mid = r.astype(BF16).astype(F32)
    return hi, mid, r - mid


def _head_cols(x, h):
    col = x[:, (h // 2) * LANES:(h // 2 + 1) * LANES]
    return pltpu.roll(col, FOX_DIM, 1) if h % 2 else col


def _fox_kv_aug(fk_head, fv_head, cum, fka_ref, fva_ref):
    rows = cum.shape[0]
    lane = lax.broadcasted_iota(jnp.int32, (rows, LANES), 1)
    for h in range(HEADS):
        c = jnp.broadcast_to(cum[:, BIAS_LANE + h:BIAS_LANE + h + 1], (rows, LANES)) * LOG2E
        hi, mid, lo = _split3(c)
        kb = jnp.where(lane < BIAS_LANE + 3, 1.0,
             jnp.where(lane == BIAS_LANE + 3, -hi,
             jnp.where(lane == BIAS_LANE + 4, -mid,
             jnp.where(lane == BIAS_LANE + 5, -lo, 0.0))))
        ka = jnp.where(lane < BIAS_LANE, fk_head(h), kb)
        va = jnp.where(lane < BIAS_LANE, fv_head(h),
                       jnp.where(lane == BIAS_LANE, 1.0, 0.0))
        fka_ref[:, h * LANES:(h + 1) * LANES] = ka.astype(BF16)
        fva_ref[:, h * LANES:(h + 1) * LANES] = va.astype(BF16)


def _mla_kv(lat, kr128, wukv_ref, k_ref, v_ref):
    rows = lat.shape[0]
    kv = jnp.dot(lat.astype(BF16), wukv_ref[...], preferred_element_type=F32)
    lane = lax.broadcasted_iota(jnp.int32, (rows, LANES), 1)
    krb = jnp.where(lane < MLA_ROPE, kr128, 0.0).astype(BF16)
    for h in range(HEADS):
        k_ref[:, h * MLA_DK:h * MLA_DK + MLA_NOPE] = kv[:, 2 * h * LANES:(2 * h + 1) * LANES].astype(BF16)
        k_ref[:, h * MLA_DK + MLA_NOPE:(h + 1) * MLA_DK] = krb
        v_ref[:, h * MLA_V:(h + 1) * MLA_V] = kv[:, (2 * h + 1) * LANES:(2 * h + 2) * LANES].astype(BF16)


def _proj_kernel(x_ref, win_ref, wuq_ref, wukv_ref, qg_ref, kvg_ref, bf_ref, cos_ref, sin_ref,
                 cin_ref, mk_ref, mv_ref,
                 lat_ref, kr_ref, fk_ref, fv_ref, logf_ref, q_ref, k_ref, v_ref,
                 fqa_ref, fka_ref, fva_ref, gmla_ref, gfox_ref, mmem_ref, carry_ref):
    tm = x_ref.shape[0]

    @pl.when(pl.program_id(1) == 0)
    def _():
        carry_ref[...] = cin_ref[0]

    xb = x_ref[...].astype(BF16)

    def seg(lo, hi):
        return jnp.dot(xb, win_ref[:, lo:hi], preferred_element_type=F32)

    cos = cos_ref[...]
    sin = sin_ref[...]
    lane = lax.broadcasted_iota(jnp.int32, (tm, LANES), 1)

    cqn = _rms(seg(O_CQ, O_CKV), qg_ref[...])
    q = jnp.dot(cqn.astype(BF16), wuq_ref[...], preferred_element_type=F32) * (MLA_SCALE * LOG2E)
    for h in range(HEADS):
        q_ref[:, h * MLA_DK:h * MLA_DK + MLA_NOPE] = q[:, 2 * h * LANES:(2 * h + 1) * LANES].astype(BF16)
        q_ref[:, h * MLA_DK + MLA_NOPE:(h + 1) * MLA_DK] = _rope128(
            q[:, (2 * h + 1) * LANES:(2 * h + 2) * LANES], cos, sin).astype(BF16)

    lat = _rms(seg(O_CKV, O_GMLA), kvg_ref[...])
    lat_ref[...] = lat
    krl = seg(O_KR, O_END)
    kr128 = _rope128(krl, cos, sin)
    kr_ref[...] = kr128[:, :MLA_ROPE]
    _mla_kv(lat, kr128, wukv_ref, k_ref, v_ref)

    z = krl + bf_ref[...]
    logf = jnp.minimum(z, 0.0) - jnp.log1p(jnp.exp(-jnp.abs(z)))
    logf = jnp.where((lane >= BIAS_LANE) & (lane < BIAS_LANE + HEADS), logf, 0.0)
    logf_ref[...] = pltpu.roll(logf, LANES - BIAS_LANE, 1)[:, :HEADS]
    cum = _cumsum_rows(logf) + carry_ref[0:1, :]
    carry_ref[...] = jnp.broadcast_to(cum[tm - 1:tm, :], carry_ref.shape)

    fq = seg(O_FQ, O_FK)
    fk = seg(O_FK, O_FV)
    fv = seg(O_FV, O_GFOX)
    fk_ref[...] = fk
    fv_ref[...] = fv
    _fox_kv_aug(lambda h: _head_cols(fk, h), lambda h: _head_cols(fv, h), cum, fka_ref, fva_ref)
    gfox = _silu(seg(O_GFOX, O_MQ))
    for h in range(HEADS):
        c = jnp.broadcast_to(cum[:, BIAS_LANE + h:BIAS_LANE + h + 1], (tm, LANES)) * LOG2E
        hi, mid, lo = _split3(c)
        qb = jnp.where(lane == BIAS_LANE, hi,
             jnp.where(lane == BIAS_LANE + 1, mid,
             jnp.where(lane == BIAS_LANE + 2, lo,
             jnp.where(lane < BIAS_LANE + 6, 1.0, 0.0))))
        qa = jnp.where(lane < BIAS_LANE, _head_cols(fq, h) * (FOX_SCALE * LOG2E), qb)
        fqa_ref[:, h * LANES:(h + 1) * LANES] = qa.astype(BF16)
        gfox_ref[:, h * LANES:(h + 1) * LANES] = jnp.where(lane < BIAS_LANE, _head_cols(gfox, h), 0.0)

    gmla_ref[...] = _silu(seg(O_GMLA, O_FQ))

    mq = seg(O_MQ, O_GMEM).astype(BF16)
    gmem = _silu(seg(O_GMEM, O_KR))
    mk = mk_ref[0]
    mv = mv_ref[0].astype(BF16)
    key_head = lax.broadcasted_iota(jnp.int32, mk.shape, 1) // MEM_DIM
    out_head = lax.broadcasted_iota(jnp.int32, (tm, HEADS * MEM_DIM), 1) // MEM_DIM
    o_mem = jnp.zeros((tm, HEADS * MEM_DIM), F32)
    for h in range(HEADS):
        mk_h = jnp.where(key_head == h, mk, 0.0).astype(BF16)
        s = lax.dot_general(mq, mk_h, (((1,), (1,)), ((), ())), preferred_element_type=F32) * MEM_SCALE
        e = jnp.exp(s - jnp.max(s, axis=-1, keepdims=True))
        p = e / jnp.sum(e, axis=-1, keepdims=True)
        o_h = jnp.dot(p.astype(BF16), mv, preferred_element_type=F32)
        o_mem = jnp.where(out_head == h, o_h, o_mem)
    mmem_ref[...] = (o_mem * gmem).astype(BF16)


KV_OUTPUTS = (6, 7, 9, 10)


def _proj(x, cin, mk, mv, cos, sin, w, *, batch, seq, tm, kv_bufs=None, kv_seq=None, kv_start=0):
    rows = batch * seq
    nt = seq // tm
    row_map = lambda b, i: (b * nt + i, 0)
    full = lambda b, i: (0, 0)
    per_b = lambda b, i: (b, 0, 0)
    tab_map = lambda b, i: (i, 0)

    def rows_spec(width):
        return pl.BlockSpec((tm, width), row_map)

    def full_spec(a):
        return pl.BlockSpec(a.shape, full, pipeline_mode=pl.Buffered(1))

    out_widths = [(MLA_KV_RANK, F32), (MLA_ROPE, F32), (HEADS * FOX_DIM, F32), (HEADS * FOX_DIM, F32),
                  (HEADS, F32), (HEADS * MLA_DK, BF16), (HEADS * MLA_DK, BF16), (HEADS * MLA_V, BF16),
                  (HEADS * FOX_DK, BF16), (HEADS * FOX_DK, BF16), (HEADS * FOX_DK, BF16),
                  (HEADS * MLA_V, F32), (HEADS * FOX_DK, F32), (HEADS * MEM_DIM, BF16)]
    out_specs = [rows_spec(wd) for wd, _ in out_widths]
    out_shape = [jax.ShapeDtypeStruct((rows, wd), dt) for wd, dt in out_widths]
    n_in = 12
    extra_in, extra_specs, aliases, kern = [], [], {}, _proj_kernel
    if kv_bufs is not None:
        kv_map = lambda b, i: (b * (kv_seq // tm) + kv_start // tm + i, 0)
        for n, (pos, buf) in enumerate(zip(KV_OUTPUTS, kv_bufs)):
            out_specs[pos] = pl.BlockSpec((tm, out_widths[pos][0]), kv_map)
            out_shape[pos] = jax.ShapeDtypeStruct(buf.shape, buf.dtype)
            aliases[n_in + n] = pos
        extra_in = list(kv_bufs)
        extra_specs = [pl.BlockSpec(memory_space=pl.ANY)] * len(kv_bufs)

        def kern(*refs):
            _proj_kernel(*refs[:n_in], *refs[n_in + len(kv_bufs):])

    return pl.pallas_call(
        kern,
        grid=(batch, nt),
        in_specs=[rows_spec(D_MODEL), full_spec(w['win']), full_spec(w['wuq']), full_spec(w['wukv']),
                  full_spec(w['qg']), full_spec(w['kvg']), full_spec(w['bf']),
                  pl.BlockSpec((tm, LANES), tab_map), pl.BlockSpec((tm, LANES), tab_map),
                  pl.BlockSpec((1, 8, LANES), per_b),
                  pl.BlockSpec((1, N_MEM, HEADS * MEM_DIM), per_b),
                  pl.BlockSpec((1, N_MEM, HEADS * MEM_DIM), per_b)] + extra_specs,
        out_specs=out_specs,
        out_shape=out_shape,
        input_output_aliases=aliases,
        scratch_shapes=[pltpu.VMEM((8, LANES), F32)],
        compiler_params=_params(("arbitrary", "arbitrary")),
        name="proj",
    )(x, w['win'], w['wuq'], w['wukv'], w['qg'], w['kvg'], w['bf'], cos, sin, cin, mk, mv, *extra_in)


def _cache_kernel(lat_ref, kr_ref, fk_ref, fv_ref, logf_ref, wukv_ref,
                  k_ref, v_ref, fka_ref, fva_ref, cout_ref, carry_ref, *, nt):
    tm = lat_ref.shape[0]
    i = pl.program_id(1)

    @pl.when(i == 0)
    def _():
        carry_ref[...] = jnp.zeros_like(carry_ref)

    @pl.when(i < nt)
    def _():
        kr128 = jnp.concatenate([kr_ref[...], jnp.zeros((tm, LANES - MLA_ROPE), F32)], axis=1)
        _mla_kv(lat_ref[...], kr128, wukv_ref, k_ref, v_ref)
        cum = _cumsum_rows(logf_ref[...]) + carry_ref[0:1, :]
        carry_ref[...] = jnp.broadcast_to(cum[tm - 1:tm, :], carry_ref.shape)
        cout_ref[0] = carry_ref[...]
        zeros = jnp.zeros((tm, LANES - FOX_DIM), F32)
        _fox_kv_aug(lambda h: jnp.concatenate([fk_ref[:, h, :], zeros], axis=1),
                    lambda h: jnp.concatenate([fv_ref[:, h, :], zeros], axis=1), cum, fka_ref, fva_ref)

    @pl.when(i >= nt)
    def _():
        for ref in (k_ref, v_ref, fka_ref, fva_ref):
            ref[...] = jnp.zeros_like(ref)


def _cache_prep(lat, kr, fk, fv, logf128, wukv, *, layer, batch, seq, seq_pad, tm):
    nt = seq // tm
    nt_pad = seq_pad // tm
    in_map = lambda b, i: (b * nt + jnp.minimum(i, nt - 1), 0)
    in_spec = lambda width: pl.BlockSpec((tm, width), in_map)
    cache_spec = lambda *minor: pl.BlockSpec(
        (None, None, tm) + minor, lambda b, i: (layer, b, jnp.minimum(i, nt - 1)) + (0,) * len(minor))
    out_spec = lambda width: pl.BlockSpec((tm, width), lambda b, i: (b * nt_pad + i, 0))
    out_widths = [HEADS * MLA_DK, HEADS * MLA_V, HEADS * FOX_DK, HEADS * FOX_DK]
    return pl.pallas_call(
        functools.partial(_cache_kernel, nt=nt),
        grid=(batch, nt_pad),
        in_specs=[cache_spec(MLA_KV_RANK), cache_spec(MLA_ROPE), cache_spec(HEADS, FOX_DIM),
                  cache_spec(HEADS, FOX_DIM), in_spec(LANES),
                  pl.BlockSpec(wukv.shape, lambda b, i: (0, 0))],
        out_specs=[out_spec(wd) for wd in out_widths]
                  + [pl.BlockSpec((1, 8, LANES), lambda b, i: (b, 0, 0))],
        out_shape=[jax.ShapeDtypeStruct((batch * seq_pad, wd), BF16) for wd in out_widths]
                  + [jax.ShapeDtypeStruct((batch, 8, LANES), F32)],
        scratch_shapes=[pltpu.VMEM((8, LANES), F32)],
        compiler_params=_params(("arbitrary", "arbitrary")),
        name="cache_prep",
    )(lat, kr, fk, fv, logf128, wukv)


def _attn_kernel(q_ref, k_ref, v_ref, g_ref, o_ref, s_ref, p_ref, m_ref, l_ref, acc_ref,
                 *, tk, dk, past, kv_len, chunk, ones_col):
    tq = q_ref.shape[0]
    dv = LANES
    rb = min(ATTN_ROW_BLOCK, tq)
    q0 = past + pl.program_id(1) * tq
    n_full = jnp.minimum((q0 // chunk + 1) * chunk, kv_len) // tk
    n_all = (jnp.minimum(((q0 + tq - 1) // chunk + 1) * chunk, kv_len) + tk - 1) // tk
    lane = lax.broadcasted_iota(jnp.int32, (rb, LANES), 1)

    m_ref[...] = jnp.full(m_ref.shape, NEG_INF, F32)
    l_ref[...] = jnp.zeros(l_ref.shape, F32)
    acc_ref[...] = jnp.zeros(acc_ref.shape, F32)

    def step(j, carry, masked):
        off = pl.multiple_of(j * tk, tk)
        for h in range(HEADS):
            s_ref[h] = lax.dot_general(q_ref[:, h * dk:(h + 1) * dk], k_ref[pl.ds(off, tk), h * dk:(h + 1) * dk],
                                       (((1,), (1,)), ((), ())), preferred_element_type=F32)
        for h in range(HEADS):
            hs = slice(h * dv, (h + 1) * dv)
            for r in range(0, tq, rb):
                rows = slice(r, r + rb)
                cols = [s_ref[h, rows, c * LANES:(c + 1) * LANES] for c in range(tk // LANES)]
                if masked:
                    qpos = q0 + r + lax.broadcasted_iota(jnp.int32, (rb, LANES), 0)
                    limit = jnp.minimum((qpos // chunk + 1) * chunk, kv_len) - off
                    cols = [jnp.where(lane + c * LANES < limit, x, NEG_INF) for c, x in enumerate(cols)]
                smax = functools.reduce(jnp.maximum, cols)
                m_old = m_ref[rows, hs]
                m_new = jnp.maximum(m_old, jnp.max(smax, axis=1, keepdims=True))
                alpha = jnp.exp2(m_old - m_new)
                ps = [jnp.exp2(x - m_new) for x in cols]
                for c, x in enumerate(ps):
                    p_ref[h, rows, c * LANES:(c + 1) * LANES] = x.astype(BF16)
                m_ref[rows, hs] = m_new
                acc_ref[rows, hs] = acc_ref[rows, hs] * alpha
                if not ones_col:
                    l_ref[rows, hs] = l_ref[rows, hs] * alpha + functools.reduce(jnp.add, ps)
            acc_ref[:, hs] += jnp.dot(p_ref[h], v_ref[pl.ds(off, tk), hs], preferred_element_type=F32)
        return carry

    lax.fori_loop(0, n_full, functools.partial(step, masked=False), 0)
    lax.fori_loop(n_full, n_all, functools.partial(step, masked=True), 0)
    for h in range(HEADS):
        hs = slice(h * dv, (h + 1) * dv)
        acc = acc_ref[:, hs]
        if ones_col:
            l = acc[:, BIAS_LANE:BIAS_LANE + 1]
        else:
            l = jnp.sum(l_ref[:, hs], axis=1, keepdims=True)
        o_ref[:, hs] = (acc / l * g_ref[:, hs]).astype(BF16)


def _attn(q, k, v, g, *, name, batch, tq_total, tk_total, tq, tk, dk, past, kv_len, chunk, ones_col):
    nq = tq_total // tq
    dv = LANES
    kern = functools.partial(_attn_kernel, tk=tk, dk=dk, past=past, kv_len=kv_len, chunk=chunk,
                             ones_col=ones_col)
    stats = pltpu.VMEM((tq, HEADS * dv), F32)
    return pl.pallas_call(
        kern,
        grid=(batch, nq),
        in_specs=[pl.BlockSpec((tq, HEADS * dk), lambda b, i: (b * nq + i, 0)),
                  pl.BlockSpec((tk_total, HEADS * dk), lambda b, i: (b, 0)),
                  pl.BlockSpec((tk_total, HEADS * dv), lambda b, i: (b, 0)),
                  pl.BlockSpec((tq, HEADS * dv), lambda b, i: (b * nq + i, 0))],
        out_specs=pl.BlockSpec((tq, HEADS * dv), lambda b, i: (b * nq + i, 0)),
        out_shape=jax.ShapeDtypeStruct((batch * tq_total, HEADS * dv), BF16),
        scratch_shapes=[pltpu.VMEM((HEADS, tq, tk), F32), pltpu.VMEM((HEADS, tq, tk), BF16),
                        stats, stats, stats],
        compiler_params=_params(("arbitrary", "arbitrary")),
        name=name,
    )(q, k, v, g)


def _out_kernel(mla_ref, fox_ref, mem_ref, x_ref, wmla_ref, wfox_ref, wmem_ref, g_ref, b_ref, y_ref):
    out = jnp.dot(mla_ref[...], wmla_ref[...], preferred_element_type=F32)
    out += jnp.dot(fox_ref[...], wfox_ref[...], preferred_element_type=F32)
    out += jnp.dot(mem_ref[...], wmem_ref[...], preferred_element_type=F32)
    z = DEEPNORM_ALPHA * x_ref[...] + out
    mu = jnp.mean(z, axis=-1, keepdims=True)
    zc = z - mu
    var = jnp.mean(zc * zc, axis=-1, keepdims=True)
    y_ref[...] = zc * lax.rsqrt(var + NORM_EPS) * g_ref[...] + b_ref[...]


def _out(mla, fox, mem, x, w, *, tm):
    rows = x.shape[0]
    rows_spec = lambda width: pl.BlockSpec((tm, width), lambda r: (r, 0))
    full_spec = lambda a: pl.BlockSpec(a.shape, lambda r: (0, 0))
    return pl.pallas_call(
        _out_kernel,
        grid=(rows // tm,),
        in_specs=[rows_spec(mla.shape[1]), rows_spec(fox.shape[1]), rows_spec(mem.shape[1]),
                  rows_spec(D_MODEL), full_spec(w['wo_mla']), full_spec(w['wo_fox']),
                  full_spec(w['wo_mem']), full_spec(w['ln_g']), full_spec(w['ln_b'])],
        out_specs=rows_spec(D_MODEL),
        out_shape=jax.ShapeDtypeStruct((rows, D_MODEL), F32),
        compiler_params=_params(("arbitrary",)),
        name="out_ln",
    )(mla, fox, mem, x, w['wo_mla'], w['wo_fox'], w['wo_mem'], w['ln_g'], w['ln_b'])


def _memkv_kernel(m_ref, w_ref, k_ref, v_ref):
    kv = jnp.dot(m_ref[...].astype(BF16), w_ref[...], preferred_element_type=F32)
    k_ref[...] = kv[:, :HEADS * MEM_DIM]
    v_ref[...] = kv[:, HEADS * MEM_DIM:]


def _mem_kv(mem, wmem):
    rows = mem.shape[0]
    width = HEADS * MEM_DIM
    return pl.pallas_call(
        _memkv_kernel,
        grid=(rows // N_MEM,),
        in_specs=[pl.BlockSpec((N_MEM, D_MODEL), lambda r: (r, 0)),
                  pl.BlockSpec(wmem.shape, lambda r: (0, 0))],
        out_specs=[pl.BlockSpec((N_MEM, width), lambda r: (r, 0))] * 2,
        out_shape=[jax.ShapeDtypeStruct((rows, width), F32)] * 2,
        compiler_params=_params(("arbitrary",)),
        name="mem_kv",
    )(mem, wmem)


def _rope_tables(pos):
    half = MLA_ROPE // 2
    inv = ROPE_THETA ** (-jnp.arange(half, dtype=F32) / half)
    ang = pos.astype(F32)[:, None] * inv[None, :]
    cos, sin = jnp.cos(ang), jnp.sin(ang)
    n = pos.shape[0]
    return (jnp.concatenate([cos, cos, jnp.ones((n, LANES - MLA_ROPE), F32)], axis=1),
            jnp.concatenate([-sin, sin, jnp.zeros((n, LANES - MLA_ROPE), F32)], axis=1))


def _layer_weights(w_in, b_f, qg, kvg, w_uq, w_ukv, w_out, ln_g, ln_b):
    c = [0]
    for n in (MLA_Q_RANK, MLA_KV_RANK, MLA_ROPE, HEADS * MLA_V, 256, 256, 256, HEADS, 256, 256, 256):
        c.append(c[-1] + n)
    s = lambda i: w_in[:, c[i]:c[i + 1]]
    win = jnp.concatenate([s(0), s(1), s(3), s(4), s(5), s(6), s(8), s(9), s(10), s(2), s(7),
                           jnp.zeros((D_MODEL, LANES - MLA_ROPE - HEADS), F32)], axis=1).astype(BF16)
    wuq = jnp.pad(w_uq.reshape(MLA_Q_RANK, HEADS, MLA_NOPE + MLA_ROPE),
                  ((0, 0), (0, 0), (0, MLA_DK - MLA_NOPE - MLA_ROPE))).reshape(MLA_Q_RANK, HEADS * MLA_DK)
    wo_fox = jnp.pad(w_out[HEADS * MLA_V:HEADS * MLA_V + HEADS * FOX_DIM].reshape(HEADS, FOX_DIM, D_MODEL),
                     ((0, 0), (0, FOX_DK - FOX_DIM), (0, 0))).reshape(HEADS * FOX_DK, D_MODEL)
    bf = jnp.zeros((1, LANES), F32).at[0, BIAS_LANE:BIAS_LANE + HEADS].set(b_f)
    return dict(win=win, wuq=wuq.astype(BF16), wukv=w_ukv.astype(BF16),
                qg=qg.reshape(1, -1), kvg=kvg.reshape(1, -1), bf=bf,
                wo_mla=w_out[:HEADS * MLA_V].astype(BF16), wo_fox=wo_fox.astype(BF16),
                wo_mem=w_out[HEADS * MLA_V + HEADS * FOX_DIM:].astype(BF16),
                ln_g=ln_g.reshape(1, -1), ln_b=ln_b.reshape(1, -1))


def kernel(x_prompt, x_sample, cache_mla_latent, cache_mla_krope, cache_fox_k, cache_fox_v, cache_fox_logf,
           cache_mem_k, cache_mem_v, mem_prompt, w_in, b_fox_f, mla_q_norm, mla_kv_norm, w_uq, w_ukv,
           w_mem_kv, w_out, ln_g, ln_b):
    batch, seq, _ = x_prompt.shape
    dec_batch, dec_seq, _ = x_sample.shape
    depth = w_in.shape[0]
    past = cache_mla_latent.shape[2]
    kv_len = past + dec_seq
    kv_pad = -(-kv_len // CACHE_TM) * CACHE_TM
    tk_s = kv_pad

    cos_p, sin_p = _rope_tables(jnp.arange(seq))
    cos_s, sin_s = _rope_tables(past + jnp.arange(dec_seq))
    zero_carry = jnp.zeros((batch, 8, LANES), F32)

    yp = x_prompt.reshape(batch * seq, D_MODEL)
    ys = x_sample.reshape(dec_batch * dec_seq, D_MODEL)
    mem2d = mem_prompt.reshape(batch * N_MEM, D_MODEL)
    p_out, s_out, p_mk, p_mv = [], [], [], []
    for l in range(depth):
        w = _layer_weights(w_in[l], b_fox_f[l], mla_q_norm[l], mla_kv_norm[l], w_uq[l], w_ukv[l],
                           w_out[l], ln_g[l], ln_b[l])
        mk, mv = _mem_kv(mem2d, w_mem_kv[l].astype(BF16))
        (lat, kr, fk, fv, logf, q, k, v, fqa, fka, fva, gmla, gfox, mmem) = _proj(
            yp, zero_carry, mk.reshape(batch, N_MEM, -1), mv.reshape(batch, N_MEM, -1), cos_p, sin_p, w,
            batch=batch, seq=seq, tm=ROW_TM)
        o_mla = _attn(q, k, v, gmla, name="attn_mla", batch=batch, tq_total=seq, tk_total=seq, tq=ATTN_TQ,
                      tk=ATTN_TK, dk=MLA_DK, past=0, kv_len=seq, chunk=CHUNK, ones_col=False)
        o_fox = _attn(fqa, fka, fva, gfox, name="attn_fox", batch=batch, tq_total=seq, tk_total=seq, tq=ATTN_TQ,
                      tk=ATTN_TK, dk=FOX_DK, past=0, kv_len=seq, chunk=1, ones_col=True)
        yp = _out(o_mla, o_fox, mmem, yp, w, tm=ROW_TM)
        p_out.append((lat, kr, fk, fv, logf))
        p_mk.append(mk)
        p_mv.append(mv)

        rows_c = dec_batch * past
        logf_c = jnp.pad(cache_fox_logf[l].reshape(rows_c, HEADS),
                         ((0, 0), (BIAS_LANE, LANES - BIAS_LANE - HEADS)))
        ck, cv, cfka, cfva, carry = _cache_prep(
            cache_mla_latent, cache_mla_krope, cache_fox_k, cache_fox_v, logf_c, w['wukv'],
            layer=l, batch=dec_batch, seq=past, seq_pad=kv_pad, tm=CACHE_TM)
        (lat, kr, fk, fv, logf, q, k, v, fqa, fka, fva, gmla, gfox, mmem) = _proj(
            ys, carry, cache_mem_k[l].reshape(dec_batch, N_MEM, -1), cache_mem_v[l].reshape(dec_batch, N_MEM, -1),
            cos_s, sin_s, w, batch=dec_batch, seq=dec_seq, tm=dec_seq,
            kv_bufs=(ck, cv, cfka, cfva), kv_seq=kv_pad, kv_start=past)
        o_mla = _attn(q, k, v, gmla, name="attn_mla_s", batch=dec_batch,
                      tq_total=dec_seq, tk_total=kv_pad, tq=dec_seq, tk=tk_s, dk=MLA_DK, past=past,
                      kv_len=kv_len, chunk=CHUNK, ones_col=False)
        o_fox = _attn(fqa, fka, fva, gfox, name="attn_fox_s", batch=dec_batch,
                      tq_total=dec_seq, tk_total=kv_pad, tq=dec_seq, tk=tk_s, dk=FOX_DK, past=past,
                      kv_len=kv_len, chunk=1, ones_col=True)
        ys = _out(o_mla, o_fox, mmem, ys, w, tm=dec_batch * dec_seq)
        s_out.append((lat, kr, fk, fv, logf))

    def stack(rows, i, shape):
        return jnp.stack([r[i].reshape(shape) for r in rows], axis=0)

    pb, sb = (batch, seq), (dec_batch, dec_seq)
    return (yp.reshape(batch, seq, D_MODEL), ys.reshape(dec_batch, dec_seq, D_MODEL),
            stack(p_out, 0, pb + (MLA_KV_RANK,)), stack(p_out, 1, pb + (MLA_ROPE,)),
            stack(p_out, 2, pb + (HEADS, FOX_DIM)), stack(p_out, 3, pb + (HEADS, FOX_DIM)),
            stack(p_out, 4, pb + (HEADS,)),
            jnp.stack([a.reshape(batch, N_MEM, HEADS, MEM_DIM) for a in p_mk], axis=0),
            jnp.stack([a.reshape(batch, N_MEM, HEADS, MEM_DIM) for a in p_mv], axis=0),
            stack(s_out, 0, sb + (MLA_KV_RANK,)), stack(s_out, 1, sb + (MLA_ROPE,)),
            stack(s_out, 2, sb + (HEADS, FOX_DIM)), stack(s_out, 3, sb + (HEADS, FOX_DIM)),
            stack(s_out, 4, sb + (HEADS,)))
```

```python
import functools

import jax
import jax.numpy as jnp
from jax import lax
from jax.experimental import pallas as pl
from jax.experimental.pallas import tpu as pltpu

F32 = jnp.float32
BF16 = jnp.bfloat16

D_MODEL = 1024
CHUNK = 64
N_MEM = 256
HEADS = 4
MLA_NOPE = 128
MLA_ROPE = 64
MLA_V = 128
MLA_Q_RANK = 384
MLA_KV_RANK = 256
FOX_DIM = 64
MEM_DIM = 64
ROPE_THETA = 10000.0
NORM_EPS = 1e-6
NEG_INF = -1e30
MLA_SCALE = (MLA_NOPE + MLA_ROPE) ** -0.5
FOX_SCALE = FOX_DIM ** -0.5
MEM_SCALE = MEM_DIM ** -0.5
DEPTH = 2
DEEPNORM_ALPHA = (2 * DEPTH) ** 0.25
LOG2E = 1.4426950408889634

LANES = 128
MLA_DK = 256
FOX_DK = 128
BIAS_LANE = 64
VMEM_LIMIT = 52 * 1024 * 1024
ATTN_ROW_BLOCK = 32
ATTN_TQ = 512
ATTN_TK = 512
ROW_TM = 512
CACHE_TM = 256

O_CQ, O_CKV, O_GMLA, O_FQ, O_FK, O_FV, O_GFOX, O_MQ, O_GMEM, O_KR, O_END = (
    0, 384, 640, 1152, 1408, 1664, 1920, 2176, 2432, 2688, 2816)


def _params(sem):
    return pltpu.CompilerParams(dimension_semantics=sem, vmem_limit_bytes=VMEM_LIMIT)


def _silu(x):
    return x * jax.nn.sigmoid(x)


def _rms(x, g):
    return x * lax.rsqrt(jnp.mean(x * x, axis=-1, keepdims=True) + NORM_EPS) * g


def _rope128(x, cos, sin):
    lane = lax.broadcasted_iota(jnp.int32, x.shape, 1)
    swapped = jnp.where(lane < MLA_ROPE // 2, pltpu.roll(x, LANES - MLA_ROPE // 2, 1),
                        pltpu.roll(x, MLA_ROPE // 2, 1))
    return x * cos + swapped * sin


def _cumsum_rows(x):
    n = x.shape[0]
    row = lax.broadcasted_iota(jnp.int32, x.shape, 0)
    shift = 1
    while shift < n:
        x = x + jnp.where(row >= shift, pltpu.roll(x, shift, 0), 0.0)
        shift *= 2
    return x


def _split3(x):
    hi = x.astype(BF16).astype(F32)
    r = x - hi
    mid = r.astype(BF16).astype(F32)
    return hi, mid, r - mid


def _head_cols(x, h):
    col = x[:, (h // 2) * LANES:(h // 2 + 1) * LANES]
    return pltpu.roll(col, FOX_DIM, 1) if h % 2 else col


def _fox_kv_aug(fk_head, fv_head, cum, fka_ref, fva_ref):
    rows = cum.shape[0]
    lane = lax.broadcasted_iota(jnp.int32, (rows, LANES), 1)
    for h in range(HEADS):
        c = jnp.broadcast_to(cum[:, BIAS_LANE + h:BIAS_LANE + h + 1], (rows, LANES)) * LOG2E
        hi, mid, lo = _split3(c)
        kb = jnp.where(lane < BIAS_LANE + 3, 1.0,
             jnp.where(lane == BIAS_LANE + 3, -hi,
             jnp.where(lane == BIAS_LANE + 4, -mid,
             jnp.where(lane == BIAS_LANE + 5, -lo, 0.0))))
        ka = jnp.where(lane < BIAS_LANE, fk_head(h), kb)
        va = jnp.where(lane < BIAS_LANE, fv_head(h),
                       jnp.where(lane == BIAS_LANE, 1.0, 0.0))
        fka_ref[:, h * LANES:(h + 1) * LANES] = ka.astype(BF16)
        fva_ref[:, h * LANES:(h + 1) * LANES] = va.astype(BF16)


def _mla_kv(lat, kr128, wukv_ref, k_ref, v_ref):
    rows = lat.shape[0]
    kv = jnp.dot(lat.astype(BF16), wukv_ref[...], preferred_element_type=F32)
    lane = lax.broadcasted_iota(jnp.int32, (rows, LANES), 1)
    krb = jnp.where(lane < MLA_ROPE, kr128, 0.0).astype(BF16)
    for h in range(HEADS):
        k_ref[:, h * MLA_DK:h * MLA_DK + MLA_NOPE] = kv[:, 2 * h * LANES:(2 * h + 1) * LANES].astype(BF16)
        k_ref[:, h * MLA_DK + MLA_NOPE:(h + 1) * MLA_DK] = krb
        v_ref[:, h * MLA_V:(h + 1) * MLA_V] = kv[:, (2 * h + 1) * LANES:(2 * h + 2) * LANES].astype(BF16)


def _proj_kernel(x_ref, win_ref, wuq_ref, wukv_ref, qg_ref, kvg_ref, bf_ref, cos_ref, sin_ref,
                 cin_ref, mk_ref, mv_ref,
                 lat_ref, kr_ref, fk_ref, fv_ref, logf_ref, q_ref, k_ref, v_ref,
                 fqa_ref, fka_ref, fva_ref, gmla_ref, gfox_ref, mmem_ref, carry_ref):
    tm = x_ref.shape[0]

    @pl.when(pl.program_id(1) == 0)
    def _():
        carry_ref[...] = cin_ref[0]

    xb = x_ref[...].astype(BF16)

    def seg(lo, hi):
        return jnp.dot(xb, win_ref[:, lo:hi], preferred_element_type=F32)

    cos = cos_ref[...]
    sin = sin_ref[...]
    lane = lax.broadcasted_iota(jnp.int32, (tm, LANES), 1)

    cqn = _rms(seg(O_CQ, O_CKV), qg_ref[...])
    q = jnp.dot(cqn.astype(BF16), wuq_ref[...], preferred_element_type=F32) * (MLA_SCALE * LOG2E)
    for h in range(HEADS):
        q_ref[:, h * MLA_DK:h * MLA_DK + MLA_NOPE] = q[:, 2 * h * LANES:(2 * h + 1) * LANES].astype(BF16)
        q_ref[:, h * MLA_DK + MLA_NOPE:(h + 1) * MLA_DK] = _rope128(
            q[:, (2 * h + 1) * LANES:(2 * h + 2) * LANES], cos, sin).astype(BF16)

    lat = _rms(seg(O_CKV, O_GMLA), kvg_ref[...])
    lat_ref[...] = lat
    krl = seg(O_KR, O_END)
    kr128 = _rope128(krl, cos, sin)
    kr_ref[...] = kr128[:, :MLA_ROPE]
    _mla_kv(lat, kr128, wukv_ref, k_ref, v_ref)

    z = krl + bf_ref[...]
    logf = jnp.minimum(z, 0.0) - jnp.log1p(jnp.exp(-jnp.abs(z)))
    logf = jnp.where((lane >= BIAS_LANE) & (lane < BIAS_LANE + HEADS), logf, 0.0)
    logf_ref[...] = pltpu.roll(logf, LANES - BIAS_LANE, 1)[:, :HEADS]
    cum = _cumsum_rows(logf) + carry_ref[0:1, :]
    carry_ref[...] = jnp.broadcast_to(cum[tm - 1:tm, :], carry_ref.shape)

    fq = seg(O_FQ, O_FK)
    fk = seg(O_FK, O_FV)
    fv = seg(O_FV, O_GFOX)
    fk_ref[...] = fk
    fv_ref[...] = fv
    _fox_kv_aug(lambda h: _head_cols(fk, h), lambda h: _head_cols(fv, h), cum, fka_ref, fva_ref)
    gfox = _silu(seg(O_GFOX, O_MQ))
    for h in range(HEADS):
        c = jnp.broadcast_to(cum[:, BIAS_LANE + h:BIAS_LANE + h + 1], (tm, LANES)) * LOG2E
        hi, mid, lo = _split3(c)
        qb = jnp.where(lane == BIAS_LANE, hi,
             jnp.where(lane == BIAS_LANE + 1, mid,
             jnp.where(lane == BIAS_LANE + 2, lo,
             jnp.where(lane < BIAS_LANE + 6, 1.0, 0.0))))
        qa = jnp.where(lane < BIAS_LANE, _head_cols(fq, h) * (FOX_SCALE * LOG2E), qb)
        fqa_ref[:, h * LANES:(h + 1) * LANES] = qa.astype(BF16)
        gfox_ref[:, h * LANES:(h + 1) * LANES] = jnp.where(lane < BIAS_LANE, _head_cols(gfox, h), 0.0)

    gmla_ref[...] = _silu(seg(O_GMLA, O_FQ))

    mq = seg(O_MQ, O_GMEM).astype(BF16)
    gmem = _silu(seg(O_GMEM, O_KR))
    mk = mk_ref[0]
    mv = mv_ref[0].astype(BF16)
    key_head = lax.broadcasted_iota(jnp.int32, mk.shape, 1) // MEM_DIM
    out_head = lax.broadcasted_iota(jnp.int32, (tm, HEADS * MEM_DIM), 1) // MEM_DIM
    o_mem = jnp.zeros((tm, HEADS * MEM_DIM), F32)
    for h in range(HEADS):
        mk_h = jnp.where(key_head == h, mk, 0.0).astype(BF16)
        s = lax.dot_general(mq, mk_h, (((1,), (1,)), ((), ())), preferred_element_type=F32) * MEM_SCALE
        e = jnp.exp(s - jnp.max(s, axis=-1, keepdims=True))
        p = e / jnp.sum(e, axis=-1, keepdims=True)
        o_h = jnp.dot(p.astype(BF16), mv, preferred_element_type=F32)
        o_mem = jnp.where(out_head == h, o_h, o_mem)
    mmem_ref[...] = (o_mem * gmem).astype(BF16)


KV_OUTPUTS = (6, 7, 9, 10)


def _proj(x, cin, mk, mv, cos, sin, w, *, batch, seq, tm, kv_bufs=None, kv_seq=None, kv_start=0):
    rows = batch * seq
    nt = seq // tm
    row_map = lambda b, i: (b * nt + i, 0)
    full = lambda b, i: (0, 0)
    per_b = lambda b, i: (b, 0, 0)
    tab_map = lambda b, i: (i, 0)

    def rows_spec(width):
        return pl.BlockSpec((tm, width), row_map)

    def full_spec(a):
        return pl.BlockSpec(a.shape, full, pipeline_mode=pl.Buffered(1))

    out_widths = [(MLA_KV_RANK, F32), (MLA_ROPE, F32), (HEADS * FOX_DIM, F32), (HEADS * FOX_DIM, F32),
                  (HEADS, F32), (HEADS * MLA_DK, BF16), (HEADS * MLA_DK, BF16), (HEADS * MLA_V, BF16),
                  (HEADS * FOX_DK, BF16), (HEADS * FOX_DK, BF16), (HEADS * FOX_DK, BF16),
                  (HEADS * MLA_V, F32), (HEADS * FOX_DK, F32), (HEADS * MEM_DIM, BF16)]
    out_specs = [rows_spec(wd) for wd, _ in out_widths]
    out_shape = [jax.ShapeDtypeStruct((rows, wd), dt) for wd, dt in out_widths]
    n_in = 12
    extra_in, extra_specs, aliases, kern = [], [], {}, _proj_kernel
    if kv_bufs is not None:
        kv_map = lambda b, i: (b * (kv_seq // tm) + kv_start // tm + i, 0)
        for n, (pos, buf) in enumerate(zip(KV_OUTPUTS, kv_bufs)):
            out_specs[pos] = pl.BlockSpec((tm, out_widths[pos][0]), kv_map)
            out_shape[pos] = jax.ShapeDtypeStruct(buf.shape, buf.dtype)
            aliases[n_in + n] = pos
        extra_in = list(kv_bufs)
        extra_specs = [pl.BlockSpec(memory_space=pl.ANY)] * len(kv_bufs)

        def kern(*refs):
            _proj_kernel(*refs[:n_in], *refs[n_in + len(kv_bufs):])

    return pl.pallas_call(
        kern,
        grid=(batch, nt),
        in_specs=[rows_spec(D_MODEL), full_spec(w['win']), full_spec(w['wuq']), full_spec(w['wukv']),
                  full_spec(w['qg']), full_spec(w['kvg']), full_spec(w['bf']),
                  pl.BlockSpec((tm, LANES), tab_map), pl.BlockSpec((tm, LANES), tab_map),
                  pl.BlockSpec((1, 8, LANES), per_b),
                  pl.BlockSpec((1, N_MEM, HEADS * MEM_DIM), per_b),
                  pl.BlockSpec((1, N_MEM, HEADS * MEM_DIM), per_b)] + extra_specs,
        out_specs=out_specs,
        out_shape=out_shape,
        input_output_aliases=aliases,
        scratch_shapes=[pltpu.VMEM((8, LANES), F32)],
        compiler_params=_params(("arbitrary", "arbitrary")),
        name="proj",
    )(x, w['win'], w['wuq'], w['wukv'], w['qg'], w['kvg'], w['bf'], cos, sin, cin, mk, mv, *extra_in)


def _cache_kernel(lat_ref, kr_ref, fk_ref, fv_ref, logf_ref, wukv_ref,
                  k_ref, v_ref, fka_ref, fva_ref, cout_ref, carry_ref, *, nt):
    tm = lat_ref.shape[0]
    i = pl.program_id(1)

    @pl.when(i == 0)
    def _():
        carry_ref[...] = jnp.zeros_like(carry_ref)

    @pl.when(i < nt)
    def _():
        kr128 = jnp.concatenate([kr_ref[...], jnp.zeros((tm, LANES - MLA_ROPE), F32)], axis=1)
        _mla_kv(lat_ref[...], kr128, wukv_ref, k_ref, v_ref)
        cum = _cumsum_rows(logf_ref[...]) + carry_ref[0:1, :]
        carry_ref[...] = jnp.broadcast_to(cum[tm - 1:tm, :], carry_ref.shape)
        cout_ref[0] = carry_ref[...]
        fk = fk_ref[...]
        fv = fv_ref[...]
        _fox_kv_aug(lambda h: _head_cols(fk, h), lambda h: _head_cols(fv, h), cum, fka_ref, fva_ref)

    @pl.when(i >= nt)
    def _():
        for ref in (k_ref, v_ref, fka_ref, fva_ref):
            ref[...] = jnp.zeros_like(ref)


def _cache_prep(lat, kr, fk, fv, logf128, wukv, *, layer, batch, seq, seq_pad, tm):
    nt = seq // tm
    nt_pad = seq_pad // tm
    in_map = lambda b, i: (b * nt + jnp.minimum(i, nt - 1), 0)
    in_spec = lambda width: pl.BlockSpec((tm, width), in_map)
    cache_spec = lambda *minor: pl.BlockSpec(
        (None, None, tm) + minor, lambda b, i: (layer, b, jnp.minimum(i, nt - 1)) + (0,) * len(minor))
    out_spec = lambda width: pl.BlockSpec((tm, width), lambda b, i: (b * nt_pad + i, 0))
    out_widths = [HEADS * MLA_DK, HEADS * MLA_V, HEADS * FOX_DK, HEADS * FOX_DK]
    return pl.pallas_call(
        functools.partial(_cache_kernel, nt=nt),
        grid=(batch, nt_pad),
        in_specs=[cache_spec(MLA_KV_RANK), cache_spec(MLA_ROPE), in_spec(HEADS * FOX_DIM),
                  in_spec(HEADS * FOX_DIM), in_spec(LANES),
                  pl.BlockSpec(wukv.shape, lambda b, i: (0, 0))],
        out_specs=[out_spec(wd) for wd in out_widths]
                  + [pl.BlockSpec((1, 8, LANES), lambda b, i: (b, 0, 0))],
        out_shape=[jax.ShapeDtypeStruct((batch * seq_pad, wd), BF16) for wd in out_widths]
                  + [jax.ShapeDtypeStruct((batch, 8, LANES), F32)],
        scratch_shapes=[pltpu.VMEM((8, LANES), F32)],
        compiler_params=_params(("arbitrary", "arbitrary")),
        name="cache_prep",
    )(lat, kr, fk, fv, logf128, wukv)


def _attn_kernel(q_ref, k_ref, v_ref, g_ref, o_ref, s_ref, p_ref, m_ref, l_ref, acc_ref,
                 *, tk, dk, past, kv_len, chunk, ones_col):
    tq = q_ref.shape[0]
    dv = LANES
    rb = min(ATTN_ROW_BLOCK, tq)
    q0 = past + pl.program_id(1) * tq
    n_full = jnp.minimum((q0 // chunk + 1) * chunk, kv_len) // tk
    n_all = (jnp.minimum(((q0 + tq - 1) // chunk + 1) * chunk, kv_len) + tk - 1) // tk
    lane = lax.broadcasted_iota(jnp.int32, (rb, LANES), 1)

    m_ref[...] = jnp.full(m_ref.shape, NEG_INF, F32)
    l_ref[...] = jnp.zeros(l_ref.shape, F32)
    acc_ref[...] = jnp.zeros(acc_ref.shape, F32)

    def step(j, carry, masked):
        off = pl.multiple_of(j * tk, tk)
        for h in range(HEADS):
            s_ref[h] = lax.dot_general(q_ref[:, h * dk:(h + 1) * dk], k_ref[pl.ds(off, tk), h * dk:(h + 1) * dk],
                                       (((1,), (1,)), ((), ())), preferred_element_type=F32)
        for h in range(HEADS):
            hs = slice(h * dv, (h + 1) * dv)
            for r in range(0, tq, rb):
                rows = slice(r, r + rb)
                cols = [s_ref[h, rows, c * LANES:(c + 1) * LANES] for c in range(tk // LANES)]
                if masked:
                    qpos = q0 + r + lax.broadcasted_iota(jnp.int32, (rb, LANES), 0)
                    limit = jnp.minimum((qpos // chunk + 1) * chunk, kv_len) - off
                    cols = [jnp.where(lane + c * LANES < limit, x, NEG_INF) for c, x in enumerate(cols)]
                smax = functools.reduce(jnp.maximum, cols)
                m_old = m_ref[rows, hs]
                m_new = jnp.maximum(m_old, jnp.max(smax, axis=1, keepdims=True))
                alpha = jnp.exp2(m_old - m_new)
                ps = [jnp.exp2(x - m_new) for x in cols]
                for c, x in enumerate(ps):
                    p_ref[h, rows, c * LANES:(c + 1) * LANES] = x.astype(BF16)
                m_ref[rows, hs] = m_new
                acc_ref[rows, hs] = acc_ref[rows, hs] * alpha
                if not ones_col:
                    l_ref[rows, hs] = l_ref[rows, hs] * alpha + functools.reduce(jnp.add, ps)
            acc_ref[:, hs] += jnp.dot(p_ref[h], v_ref[pl.ds(off, tk), hs], preferred_element_type=F32)
        return carry

    lax.fori_loop(0, n_full, functools.partial(step, masked=False), 0)
    lax.fori_loop(n_full, n_all, functools.partial(step, masked=True), 0)
    for h in range(HEADS):
        hs = slice(h * dv, (h + 1) * dv)
        acc = acc_ref[:, hs]
        if ones_col:
            l = acc[:, BIAS_LANE:BIAS_LANE + 1]
        else:
            l = jnp.sum(l_ref[:, hs], axis=1, keepdims=True)
        o_ref[:, hs] = (acc / l * g_ref[:, hs]).astype(BF16)


def _attn(q, k, v, g, *, name, batch, tq_total, tk_total, tq, tk, dk, past, kv_len, chunk, ones_col):
    nq = tq_total // tq
    dv = LANES
    kern = functools.partial(_attn_kernel, tk=tk, dk=dk, past=past, kv_len=kv_len, chunk=chunk,
                             ones_col=ones_col)
    stats = pltpu.VMEM((tq, HEADS * dv), F32)
    return pl.pallas_call(
        kern,
        grid=(batch, nq),
        in_specs=[pl.BlockSpec((tq, HEADS * dk), lambda b, i: (b * nq + i, 0)),
                  pl.BlockSpec((tk_total, HEADS * dk), lambda b, i: (b, 0)),
                  pl.BlockSpec((tk_total, HEADS * dv), lambda b, i: (b, 0)),
                  pl.BlockSpec((tq, HEADS * dv), lambda b, i: (b * nq + i, 0))],
        out_specs=pl.BlockSpec((tq, HEADS * dv), lambda b, i: (b * nq + i, 0)),
        out_shape=jax.ShapeDtypeStruct((batch * tq_total, HEADS * dv), BF16),
        scratch_shapes=[pltpu.VMEM((HEADS, tq, tk), F32), pltpu.VMEM((HEADS, tq, tk), BF16),
                        stats, stats, stats],
        compiler_params=_params(("arbitrary", "arbitrary")),
        name=name,
    )(q, k, v, g)


def _out_kernel(mla_ref, fox_ref, mem_ref, x_ref, wmla_ref, wfox_ref, wmem_ref, g_ref, b_ref, y_ref):
    out = jnp.dot(mla_ref[...], wmla_ref[...], preferred_element_type=F32)
    out += jnp.dot(fox_ref[...], wfox_ref[...], preferred_element_type=F32)
    out += jnp.dot(mem_ref[...], wmem_ref[...], preferred_element_type=F32)
    z = DEEPNORM_ALPHA * x_ref[...] + out
    mu = jnp.mean(z, axis=-1, keepdims=True)
    zc = z - mu
    var = jnp.mean(zc * zc, axis=-1, keepdims=True)
    y_ref[...] = zc * lax.rsqrt(var + NORM_EPS) * g_ref[...] + b_ref[...]


def _out(mla, fox, mem, x, w, *, tm):
    rows = x.shape[0]
    rows_spec = lambda width: pl.BlockSpec((tm, width), lambda r: (r, 0))
    full_spec = lambda a: pl.BlockSpec(a.shape, lambda r: (0, 0))
    return pl.pallas_call(
        _out_kernel,
        grid=(rows // tm,),
        in_specs=[rows_spec(mla.shape[1]), rows_spec(fox.shape[1]), rows_spec(mem.shape[1]),
                  rows_spec(D_MODEL), full_spec(w['wo_mla']), full_spec(w['wo_fox']),
                  full_spec(w['wo_mem']), full_spec(w['ln_g']), full_spec(w['ln_b'])],
        out_specs=rows_spec(D_MODEL),
        out_shape=jax.ShapeDtypeStruct((rows, D_MODEL), F32),
        compiler_params=_params(("arbitrary",)),
        name="out_ln",
    )(mla, fox, mem, x, w['wo_mla'], w['wo_fox'], w['wo_mem'], w['ln_g'], w['ln_b'])


def _memkv_kernel(m_ref, w_ref, k_ref, v_ref):
    kv = jnp.dot(m_ref[...].astype(BF16), w_ref[...], preferred_element_type=F32)
    k_ref[...] = kv[:, :HEADS * MEM_DIM]
    v_ref[...] = kv[:, HEADS * MEM_DIM:]


def _mem_kv(mem, wmem):
    rows = mem.shape[0]
    width = HEADS * MEM_DIM
    return pl.pallas_call(
        _memkv_kernel,
        grid=(rows // N_MEM,),
        in_specs=[pl.BlockSpec((N_MEM, D_MODEL), lambda r: (r, 0)),
                  pl.BlockSpec(wmem.shape, lambda r: (0, 0))],
        out_specs=[pl.BlockSpec((N_MEM, width), lambda r: (r, 0))] * 2,
        out_shape=[jax.ShapeDtypeStruct((rows, width), F32)] * 2,
        compiler_params=_params(("arbitrary",)),
        name="mem_kv",
    )(mem, wmem)


def _rope_tables(pos):
    half = MLA_ROPE // 2
    inv = ROPE_THETA ** (-jnp.arange(half, dtype=F32) / half)
    ang = pos.astype(F32)[:, None] * inv[None, :]
    cos, sin = jnp.cos(ang), jnp.sin(ang)
    n = pos.shape[0]
    return (jnp.concatenate([cos, cos, jnp.ones((n, LANES - MLA_ROPE), F32)], axis=1),
            jnp.concatenate([-sin, sin, jnp.zeros((n, LANES - MLA_ROPE), F32)], axis=1))


def _layer_weights(w_in, b_f, qg, kvg, w_uq, w_ukv, w_out, ln_g, ln_b):
    c = [0]
    for n in (MLA_Q_RANK, MLA_KV_RANK, MLA_ROPE, HEADS * MLA_V, 256, 256, 256, HEADS, 256, 256, 256):
        c.append(c[-1] + n)
    s = lambda i: w_in[:, c[i]:c[i + 1]]
    win = jnp.concatenate([s(0), s(1), s(3), s(4), s(5), s(6), s(8), s(9), s(10), s(2), s(7),
                           jnp.zeros((D_MODEL, LANES - MLA_ROPE - HEADS), F32)], axis=1).astype(BF16)
    wuq = jnp.pad(w_uq.reshape(MLA_Q_RANK, HEADS, MLA_NOPE + MLA_ROPE),
                  ((0, 0), (0, 0), (0, MLA_DK - MLA_NOPE - MLA_ROPE))).reshape(MLA_Q_RANK, HEADS * MLA_DK)
    wo_fox = jnp.pad(w_out[HEADS * MLA_V:HEADS * MLA_V + HEADS * FOX_DIM].reshape(HEADS, FOX_DIM, D_MODEL),
                     ((0, 0), (0, FOX_DK - FOX_DIM), (0, 0))).reshape(HEADS * FOX_DK, D_MODEL)
    bf = jnp.zeros((1, LANES), F32).at[0, BIAS_LANE:BIAS_LANE + HEADS].set(b_f)
    return dict(win=win, wuq=wuq.astype(BF16), wukv=w_ukv.astype(BF16),
                qg=qg.reshape(1, -1), kvg=kvg.reshape(1, -1), bf=bf,
                wo_mla=w_out[:HEADS * MLA_V].astype(BF16), wo_fox=wo_fox.astype(BF16),
                wo_mem=w_out[HEADS * MLA_V + HEADS * FOX_DIM:].astype(BF16),
                ln_g=ln_g.reshape(1, -1), ln_b=ln_b.reshape(1, -1))


def kernel(x_prompt, x_sample, cache_mla_latent, cache_mla_krope, cache_fox_k, cache_fox_v, cache_fox_logf,
           cache_mem_k, cache_mem_v, mem_prompt, w_in, b_fox_f, mla_q_norm, mla_kv_norm, w_uq, w_ukv,
           w_mem_kv, w_out, ln_g, ln_b):
    batch, seq, _ = x_prompt.shape
    dec_batch, dec_seq, _ = x_sample.shape
    depth = w_in.shape[0]
    past = cache_mla_latent.shape[2]
    kv_len = past + dec_seq
    kv_pad = -(-kv_len // CACHE_TM) * CACHE_TM
    tk_s = kv_pad

    cos_p, sin_p = _rope_tables(jnp.arange(seq))
    cos_s, sin_s = _rope_tables(past + jnp.arange(dec_seq))
    zero_carry = jnp.zeros((batch, 8, LANES), F32)

    yp = x_prompt.reshape(batch * seq, D_MODEL)
    ys = x_sample.reshape(dec_batch * dec_seq, D_MODEL)
    mem2d = mem_prompt.reshape(batch * N_MEM, D_MODEL)
    p_out, s_out, p_mk, p_mv = [], [], [], []
    for l in range(depth):
        w = _layer_weights(w_in[l], b_fox_f[l], mla_q_norm[l], mla_kv_norm[l], w_uq[l], w_ukv[l],
                           w_out[l], ln_g[l], ln_b[l])
        mk, mv = _mem_kv(mem2d, w_mem_kv[l].astype(BF16))
        (lat, kr, fk, fv, logf, q, k, v, fqa, fka, fva, gmla, gfox, mmem) = _proj(
            yp, zero_carry, mk.reshape(batch, N_MEM, -1), mv.reshape(batch, N_MEM, -1), cos_p, sin_p, w,
            batch=batch, seq=seq, tm=ROW_TM)
        o_mla = _attn(q, k, v, gmla, name="attn_mla", batch=batch, tq_total=seq, tk_total=seq, tq=ATTN_TQ,
                      tk=ATTN_TK, dk=MLA_DK, past=0, kv_len=seq, chunk=CHUNK, ones_col=False)
        o_fox = _attn(fqa, fka, fva, gfox, name="attn_fox", batch=batch, tq_total=seq, tk_total=seq, tq=ATTN_TQ,
                      tk=ATTN_TK, dk=FOX_DK, past=0, kv_len=seq, chunk=1, ones_col=True)
        yp = _out(o_mla, o_fox, mmem, yp, w, tm=ROW_TM)
        p_out.append((lat, kr, fk, fv, logf))
        p_mk.append(mk)
        p_mv.append(mv)

        rows_c = dec_batch * past
        logf_c = jnp.pad(cache_fox_logf[l].reshape(rows_c, HEADS),
                         ((0, 0), (BIAS_LANE, LANES - BIAS_LANE - HEADS)))
        ck, cv, cfka, cfva, carry = _cache_prep(
            cache_mla_latent, cache_mla_krope, cache_fox_k[l].reshape(rows_c, -1),
            cache_fox_v[l].reshape(rows_c, -1), logf_c, w['wukv'],
            layer=l, batch=dec_batch, seq=past, seq_pad=kv_pad, tm=CACHE_TM)
        (lat, kr, fk, fv, logf, q, k, v, fqa, fka, fva, gmla, gfox, mmem) = _proj(
            ys, carry, cache_mem_k[l].reshape(dec_batch, N_MEM, -1), cache_mem_v[l].reshape(dec_batch, N_MEM, -1),
            cos_s, sin_s, w, batch=dec_batch, seq=dec_seq, tm=dec_seq,
            kv_bufs=(ck, cv, cfka, cfva), kv_seq=kv_pad, kv_start=past)
        o_mla = _attn(q, k, v, gmla, name="attn_mla_s", batch=dec_batch,
                      tq_total=dec_seq, tk_total=kv_pad, tq=dec_seq, tk=tk_s, dk=MLA_DK, past=past,
                      kv_len=kv_len, chunk=CHUNK, ones_col=False)
        o_fox = _attn(fqa, fka, fva, gfox, name="attn_fox_s", batch=dec_batch,
                      tq_total=dec_seq, tk_total=kv_pad, tq=dec_seq, tk=tk_s, dk=FOX_DK, past=past,
                      kv_len=kv_len, chunk=1, ones_col=True)
        ys = _out(o_mla, o_fox, mmem, ys, w, tm=dec_batch * dec_seq)
        s_out.append((lat, kr, fk, fv, logf))

    def stack(rows, i, shape):
        return jnp.stack([r[i].reshape(shape) for r in rows], axis=0)

    pb, sb = (batch, seq), (dec_batch, dec_seq)
    return (yp.reshape(batch, seq, D_MODEL), ys.reshape(dec_batch, dec_seq, D_MODEL),
            stack(p_out, 0, pb + (MLA_KV_RANK,)), stack(p_out, 1, pb + (MLA_ROPE,)),
            stack(p_out, 2, pb + (HEADS, FOX_DIM)), stack(p_out, 3, pb + (HEADS, FOX_DIM)),
            stack(p_out, 4, pb + (HEADS,)),
            jnp.stack([a.reshape(batch, N_MEM, HEADS, MEM_DIM) for a in p_mk], axis=0),
            jnp.stack([a.reshape(batch, N_MEM, HEADS, MEM_DIM) for a in p_mv], axis=0),
            stack(s_out, 0, sb + (MLA_KV_RANK,)), stack(s_out, 1, sb + (MLA_ROPE,)),
            stack(s_out, 2, sb + (HEADS, FOX_DIM)), stack(s_out, 3, sb + (HEADS, FOX_DIM)),
            stack(s_out, 4, sb + (HEADS,)))
```
